```python
import math
import jax
import jax.numpy as jnp
from jax import lax
import numpy as np

D_MODEL = 2048
BATCH = 8
SEQ = 2048
DEPTH = 1
DEC_BATCH = 16
DEC_SEQ = 64
PAST_LEN = 4096

CHUNK = 64
SSM_GROUP = 16
SSM_WIDTH = 1024
SSM_GROUPS = SSM_WIDTH // SSM_GROUP
SSM_STATE = 64
DT_MIN = 0.001
DT_MAX = 0.1
SWA_HEADS = 16
SWA_KV_HEADS = 4
SWA_HEAD_DIM = 64
SWA_GROUP = SWA_HEADS // SWA_KV_HEADS
SWA_WIDTH = SWA_HEADS * SWA_HEAD_DIM
SWA_KV_WIDTH = SWA_KV_HEADS * SWA_HEAD_DIM
WINDOW = 128
REL_BUCKETS = 32
REL_MAX_DIST = 128
MEM_LEN = 256
X_HEADS = 4
X_HEAD_DIM = 128
X_WIDTH = X_HEADS * X_HEAD_DIM
N_EXPERTS = 32
TOP_K = 4
D_FF = 2048
SWIGLU_LIMIT = 7.0
SWIGLU_ALPHA = 1.702
RMS_EPS = 1e-5
IN_COLS = SSM_WIDTH + SWA_WIDTH + 2 * SWA_KV_WIDTH + 2 * D_MODEL
IN_SPLITS = (SSM_WIDTH, SSM_WIDTH + SWA_WIDTH, SSM_WIDTH + SWA_WIDTH + SWA_KV_WIDTH, SSM_WIDTH + SWA_WIDTH + 2 * SWA_KV_WIDTH, SSM_WIDTH + SWA_WIDTH + 2 * SWA_KV_WIDTH + D_MODEL)

kernel_name = 'hybrid_s5_swa_moe_stream_step'


def rmsnorm(x, g):
    x32 = x.astype(jnp.float32)
    y = x32 * lax.rsqrt(jnp.mean(x32 * x32, axis=-1, keepdims=True) + RMS_EPS)
    return (y * g.astype(jnp.float32)).astype(x.dtype)


def complex_affine_combine(e1, e2):
    a1r, a1i, b1r, b1i = e1
    a2r, a2i, b2r, b2i = e2
    ar = a1r * a2r - a1i * a2i
    ai = a1r * a2i + a1i * a2r
    br = a2r * b1r - a2i * b1i + b2r
    bi = a2r * b1i + a2i * b1r + b2i
    return ar, ai, br, bi


def s5_branch(u, s0_re, s0_im, lam_re, lam_im, log_dt, b_re, b_im, c_re, c_im, d_skip, w_glu):
    f32 = jnp.float32
    bsz, L, _ = u.shape
    u32 = u.astype(f32).reshape(bsz, L, SSM_GROUPS, SSM_GROUP)
    lr = lam_re.astype(f32)
    li = lam_im.astype(f32)
    dt = jnp.exp(log_dt.astype(f32))[:, None]
    mag = jnp.exp(lr * dt)
    a_re = mag * jnp.cos(li * dt)
    a_im = mag * jnp.sin(li * dt)
    den = lr * lr + li * li
    f_re = ((a_re - 1.0) * lr + a_im * li) / den
    f_im = (a_im * lr - (a_re - 1.0) * li) / den
    br = b_re.astype(f32)
    bi = b_im.astype(f32)
    bb_re = f_re[..., None] * br - f_im[..., None] * bi
    bb_im = f_re[..., None] * bi + f_im[..., None] * br
    bu_re = jnp.einsum('blgh,gph->blgp', u32, bb_re)
    bu_im = jnp.einsum('blgh,gph->blgp', u32, bb_im)
    s0r = s0_re.astype(f32)
    s0i = s0_im.astype(f32)
    bu_re = bu_re.at[:, 0].add(a_re * s0r - a_im * s0i)
    bu_im = bu_im.at[:, 0].add(a_re * s0i + a_im * s0r)
    ar = jnp.broadcast_to(a_re, bu_re.shape)
    ai = jnp.broadcast_to(a_im, bu_re.shape)
    _, _, s_re, s_im = lax.associative_scan(complex_affine_combine, (ar, ai, bu_re, bu_im), axis=1)
    y = (jnp.einsum('blgp,ghp->blgh', s_re, c_re.astype(f32))
         - jnp.einsum('blgp,ghp->blgh', s_im, c_im.astype(f32))
         + d_skip.astype(f32).reshape(SSM_GROUPS, SSM_GROUP) * u32)
    y = jax.nn.gelu(y.reshape(bsz, L, SSM_WIDTH)).astype(u.dtype)
    gl = y @ w_glu
    out = gl[..., :D_MODEL] * jax.nn.sigmoid(gl[..., D_MODEL:])
    return out, s_re[:, -1].astype(s0_re.dtype), s_im[:, -1].astype(s0_im.dtype)


def t5_bucket(rel):
    half = REL_BUCKETS // 2
    max_exact = half // 2
    n = jnp.abs(rel)
    nf = jnp.maximum(n, 1).astype(jnp.float32)
    large = max_exact + (jnp.log(nf / max_exact) / math.log(REL_MAX_DIST / max_exact) * (half - max_exact)).astype(jnp.int32)
    large = jnp.minimum(large, half - 1)
    return (rel > 0).astype(jnp.int32) * half + jnp.where(n < max_exact, n, large)


def rel_bias_logits(rel_bias, n_q, n_k, n_prev):
    i = jnp.arange(n_q, dtype=jnp.int32)[:, None]
    j = jnp.arange(n_k, dtype=jnp.int32)[None, :]
    b = rel_bias[t5_bucket(j - n_prev - i)]
    return jnp.transpose(b, (2, 0, 1)).reshape(SWA_KV_HEADS, SWA_GROUP, n_q, n_k).astype(jnp.float32)


def swa_branch(q, k, v, k_prev, v_prev, has_past, sinks, rel_bias, w_o):
    bsz, L = q.shape[0], q.shape[1]
    n_prev = k_prev.shape[1]
    qb = min(L, CHUNK)
    nb = L // qb
    n_k = n_prev + qb
    k_ext = jnp.concatenate([k_prev, k], axis=1)
    v_ext = jnp.concatenate([v_prev, v], axis=1)
    idx = jnp.arange(nb)[:, None] * qb + jnp.arange(n_k)[None, :]
    kb = k_ext[:, idx]
    vb = v_ext[:, idx]
    qg = q.reshape(bsz, nb, qb, SWA_KV_HEADS, SWA_GROUP, SWA_HEAD_DIM)
    logits = jnp.einsum('bcqkgd,bcjkd->bckgqj', qg, kb).astype(jnp.float32) * (1.0 / math.sqrt(SWA_HEAD_DIM))
    logits = logits + rel_bias_logits(rel_bias, qb, n_k, n_prev)[None, None]
    if not has_past:
        valid = idx >= n_prev
        logits = jnp.where(valid[None, :, None, None, None, :], logits, -1e30)
    sink = sinks.astype(jnp.float32).reshape(SWA_KV_HEADS, SWA_GROUP)[None, None, :, :, None, None]
    m = jnp.maximum(jnp.max(logits, axis=-1, keepdims=True), sink)
    p = jnp.exp(logits - m)
    probs = p / (jnp.sum(p, axis=-1, keepdims=True) + jnp.exp(sink - m))
    out = jnp.einsum('bckgqj,bcjkd->bcqkgd', probs.astype(vb.dtype), vb).reshape(bsz, L, SWA_WIDTH)
    return out @ w_o, k_ext[:, -WINDOW:], v_ext[:, -WINDOW:]


def token_mixers(xn, s0_re, s0_im, k_prev, v_prev, has_past, w_in, ssm_p, attn_p, w_out):
    bsz, L, _ = xn.shape
    u, q, k, v, g_ssm, g_attn = jnp.split(xn @ w_in, IN_SPLITS, axis=-1)
    y_ssm, s_re, s_im = s5_branch(u, s0_re, s0_im, *ssm_p)
    q = q.reshape(bsz, L, SWA_HEADS, SWA_HEAD_DIM)
    k = k.reshape(bsz, L, SWA_KV_HEADS, SWA_HEAD_DIM)
    v = v.reshape(bsz, L, SWA_KV_HEADS, SWA_HEAD_DIM)
    y_attn, k_new, v_new = swa_branch(q, k, v, k_prev, v_prev, has_past, *attn_p)
    merged = jax.nn.sigmoid(g_ssm) * y_ssm + jax.nn.sigmoid(g_attn) * y_attn
    return merged @ w_out, s_re, s_im, k_new, v_new


def cross_attn(xn, mk, mv, w_q, w_o):
    bsz, L, _ = xn.shape
    q = (xn @ w_q).reshape(bsz, L, X_HEADS, X_HEAD_DIM)
    logits = jnp.einsum('blhd,bmhd->bhlm', q, mk).astype(jnp.float32) * (1.0 / math.sqrt(X_HEAD_DIM))
    probs = jax.nn.softmax(logits, axis=-1)
    out = jnp.einsum('bhlm,bmhd->blhd', probs.astype(mv.dtype), mv).reshape(bsz, L, X_WIDTH)
    return out @ w_o


def moe(t, w_router, b_router, w_up, b_up, w_down, b_down):
    logits = (t @ w_router + b_router).astype(jnp.float32)
    top_val, top_idx = lax.top_k(logits, TOP_K)
    gate = jax.nn.softmax(top_val, axis=-1)
    gate_full = jnp.einsum('tk,tke->te', gate, jax.nn.one_hot(top_idx, N_EXPERTS, dtype=jnp.float32)).astype(t.dtype)
    out = jnp.zeros_like(t)
    for e in range(N_EXPERTS):
        h = t @ w_up[e] + b_up[e]
        glu = jnp.minimum(h[:, :D_FF], SWIGLU_LIMIT)
        lin = jnp.clip(h[:, D_FF:], -SWIGLU_LIMIT, SWIGLU_LIMIT)
        act = glu * jax.nn.sigmoid(SWIGLU_ALPHA * glu) * (lin + 1.0)
        out = out + gate_full[:, e:e + 1] * (act @ w_down[e] + b_down[e])
    return out


def setup_inputs(seed: int = 0) -> dict:
    key = jax.random.key(seed)
    ks = iter(jax.random.split(key, 48))
    f32 = jnp.float32

    def nrm(shape, scale):
        return jax.random.normal(next(ks), shape, f32) * scale

    def gain(shape):
        return 1.0 + nrm(shape, 0.01)

    swa_len = min(WINDOW, PAST_LEN)
    lam_im_base = jnp.broadcast_to(jnp.pi * jnp.arange(SSM_STATE, dtype=f32), (DEPTH, SSM_GROUPS, SSM_STATE))
    return {
        'x_prompt': nrm((BATCH, SEQ, D_MODEL), 1.0),
        'x_sample': nrm((DEC_BATCH, DEC_SEQ, D_MODEL), 1.0),
        'state_ssm_re': nrm((DEPTH, DEC_BATCH, SSM_GROUPS, SSM_STATE), 0.1),
        'state_ssm_im': nrm((DEPTH, DEC_BATCH, SSM_GROUPS, SSM_STATE), 0.1),
        'cache_swa_k': nrm((DEPTH, DEC_BATCH, swa_len, SWA_KV_HEADS, SWA_HEAD_DIM), 1.0),
        'cache_swa_v': nrm((DEPTH, DEC_BATCH, swa_len, SWA_KV_HEADS, SWA_HEAD_DIM), 1.0),
        'cache_mem_k': nrm((DEPTH, DEC_BATCH, MEM_LEN, X_HEADS, X_HEAD_DIM), 1.0),
        'cache_mem_v': nrm((DEPTH, DEC_BATCH, MEM_LEN, X_HEADS, X_HEAD_DIM), 1.0),
        'mem_prompt': nrm((BATCH, MEM_LEN, D_MODEL), 1.0),
        'w_in': nrm((DEPTH, D_MODEL, IN_COLS), D_MODEL ** -0.5),
        'ssm_lam_re': -0.5 + nrm((DEPTH, SSM_GROUPS, SSM_STATE), 0.01),
        'ssm_lam_im': lam_im_base + nrm((DEPTH, SSM_GROUPS, SSM_STATE), 0.01),
        'ssm_log_dt': jax.random.uniform(next(ks), (DEPTH, SSM_GROUPS), f32, math.log(DT_MIN), math.log(DT_MAX)),
        'ssm_b_re': nrm((DEPTH, SSM_GROUPS, SSM_STATE, SSM_GROUP), (2 * SSM_GROUP) ** -0.5),
        'ssm_b_im': nrm((DEPTH, SSM_GROUPS, SSM_STATE, SSM_GROUP), (2 * SSM_GROUP) ** -0.5),
        'ssm_c_re': nrm((DEPTH, SSM_GROUPS, SSM_GROUP, SSM_STATE), (2 * SSM_STATE) ** -0.5),
        'ssm_c_im': nrm((DEPTH, SSM_GROUPS, SSM_GROUP, SSM_STATE), (2 * SSM_STATE) ** -0.5),
        'ssm_d': nrm((DEPTH, SSM_WIDTH), 1.0),
        'w_glu': nrm((DEPTH, SSM_WIDTH, 2 * D_MODEL), SSM_WIDTH ** -0.5),
        'swa_sinks': nrm((DEPTH, SWA_HEADS), 1.0),
        'rel_bias': nrm((REL_BUCKETS, SWA_HEADS), 0.5),
        'w_o_attn': nrm((DEPTH, SWA_WIDTH, D_MODEL), SWA_WIDTH ** -0.5),
        'w_out': nrm((DEPTH, D_MODEL, D_MODEL), D_MODEL ** -0.5),
        'w_xq': nrm((DEPTH, D_MODEL, X_WIDTH), D_MODEL ** -0.5),
        'w_mk': nrm((DEPTH, D_MODEL, X_WIDTH), D_MODEL ** -0.5),
        'w_mv': nrm((DEPTH, D_MODEL, X_WIDTH), D_MODEL ** -0.5),
        'w_xo': nrm((DEPTH, X_WIDTH, D_MODEL), X_WIDTH ** -0.5),
        'w_router': nrm((DEPTH, D_MODEL, N_EXPERTS), D_MODEL ** -0.5),
        'b_router': nrm((DEPTH, N_EXPERTS), 0.01),
        'w_up': nrm((DEPTH, N_EXPERTS, D_MODEL, 2 * D_FF), D_MODEL ** -0.5),
        'b_up': nrm((DEPTH, N_EXPERTS, 2 * D_FF), 0.01),
        'w_down': nrm((DEPTH, N_EXPERTS, D_FF, D_MODEL), D_FF ** -0.5),
        'b_down': nrm((DEPTH, N_EXPERTS, D_MODEL), 0.01),
        'norm_mix': gain((DEPTH, D_MODEL)),
        'norm_x': gain((DEPTH, D_MODEL)),
        'norm_mem': gain((DEPTH, D_MODEL)),
        'norm_moe': gain((DEPTH, D_MODEL)),
        'norm_final': gain((D_MODEL,)),
    }


def reference(x_prompt, x_sample, state_ssm_re, state_ssm_im, cache_swa_k, cache_swa_v, cache_mem_k, cache_mem_v, mem_prompt, w_in, ssm_lam_re, ssm_lam_im, ssm_log_dt, ssm_b_re, ssm_b_im, ssm_c_re, ssm_c_im, ssm_d, w_glu, swa_sinks, rel_bias, w_o_attn, w_out, w_xq, w_mk, w_mv, w_xo, w_router, b_router, w_up, b_up, w_down, b_down, norm_mix, norm_x, norm_mem, norm_moe, norm_final):
    xp, xs = x_prompt, x_sample
    bp = xp.shape[0]
    n_p = xp.shape[0] * xp.shape[1]
    sre_p, sim_p, k_p, v_p, mk_p, mv_p = [], [], [], [], [], []
    sre_s, sim_s, k_s, v_s = [], [], [], []
    for l in range(DEPTH):
        ssm_p = (ssm_lam_re[l], ssm_lam_im[l], ssm_log_dt[l], ssm_b_re[l], ssm_b_im[l], ssm_c_re[l], ssm_c_im[l], ssm_d[l], w_glu[l])
        attn_p = (swa_sinks[l], rel_bias, w_o_attn[l])
        zero_s = jnp.zeros((bp, SSM_GROUPS, SSM_STATE), jnp.float32)
        zero_kv = jnp.zeros((bp, WINDOW, SWA_KV_HEADS, SWA_HEAD_DIM), xp.dtype)
        dp, a1, a2, a3, a4 = token_mixers(rmsnorm(xp, norm_mix[l]), zero_s, zero_s, zero_kv, zero_kv, False, w_in[l], ssm_p, attn_p, w_out[l])
        ds, b1, b2, b3, b4 = token_mixers(rmsnorm(xs, norm_mix[l]), state_ssm_re[l], state_ssm_im[l], cache_swa_k[l], cache_swa_v[l], True, w_in[l], ssm_p, attn_p, w_out[l])
        xp = xp + dp
        xs = xs + ds
        sre_p.append(a1); sim_p.append(a2); k_p.append(a3); v_p.append(a4)
        sre_s.append(b1); sim_s.append(b2); k_s.append(b3); v_s.append(b4)
        mem_n = rmsnorm(mem_prompt, norm_mem[l])
        mk = (mem_n @ w_mk[l]).reshape(bp, -1, X_HEADS, X_HEAD_DIM)
        mv = (mem_n @ w_mv[l]).reshape(bp, -1, X_HEADS, X_HEAD_DIM)
        mk_p.append(mk); mv_p.append(mv)
        xp = xp + cross_attn(rmsnorm(xp, norm_x[l]), mk, mv, w_xq[l], w_xo[l])
        xs = xs + cross_attn(rmsnorm(xs, norm_x[l]), cache_mem_k[l], cache_mem_v[l], w_xq[l], w_xo[l])
        tok = jnp.concatenate([xp.reshape(-1, D_MODEL), xs.reshape(-1, D_MODEL)], axis=0)
        tok = tok + moe(rmsnorm(tok, norm_moe[l]), w_router[l], b_router[l], w_up[l], b_up[l], w_down[l], b_down[l])
        xp = tok[:n_p].reshape(xp.shape)
        xs = tok[n_p:].reshape(xs.shape)
    y_prompt = rmsnorm(xp, norm_final)
    y_sample = rmsnorm(xs, norm_final)
    new_ssm_re_prompt = jnp.stack(sre_p, axis=0)
    new_ssm_im_prompt = jnp.stack(sim_p, axis=0)
    new_swa_k_prompt = jnp.stack(k_p, axis=0)
    new_swa_v_prompt = jnp.stack(v_p, axis=0)
    new_mem_k_prompt = jnp.stack(mk_p, axis=0)
    new_mem_v_prompt = jnp.stack(mv_p, axis=0)
    new_ssm_re_sample = jnp.stack(sre_s, axis=0)
    new_ssm_im_sample = jnp.stack(sim_s, axis=0)
    new_swa_k_sample = jnp.stack(k_s, axis=0)
    new_swa_v_sample = jnp.stack(v_s, axis=0)
    return (y_prompt, y_sample, new_ssm_re_prompt, new_ssm_im_prompt, new_swa_k_prompt, new_swa_v_prompt, new_mem_k_prompt, new_mem_v_prompt, new_ssm_re_sample, new_ssm_im_sample, new_swa_k_sample, new_swa_v_sample)
```

```python
import functools
import math

import numpy as np
import jax
import jax.numpy as jnp
from jax import lax
from jax.experimental import pallas as pl
from jax.experimental.pallas import tpu as pltpu

F32 = jnp.float32
BF16 = jnp.bfloat16

LANE = 128
SUBLANE = 8
VMEM_LIMIT = 56 * 1024 * 1024

CHUNK = 64
SSM_GROUP = 16
SWA_GROUP = 4
REL_BUCKETS = 32
REL_MAX_DIST = 128
TOP_K = 4
SWIGLU_LIMIT = 7.0
SWIGLU_ALPHA = 1.702
RMS_EPS = 1e-5
SEQ_PER_SCAN = SUBLANE

S5_STRIP = 512
MOE_TM = 1024
MOE_SB = 256
MOE_TF = 256
TOK_TM = 256
ROUTER_TM = 512


def _params(sem, vmem=VMEM_LIMIT):
    return pltpu.CompilerParams(dimension_semantics=sem, vmem_limit_bytes=vmem)


def _rms(x, g):
    return x * lax.rsqrt(jnp.mean(x * x, axis=-1, keepdims=True) + RMS_EPS) * g


def _resident(shape):
    nd = len(shape)
    return pl.BlockSpec(shape, lambda *_: (0,) * nd, pipeline_mode=pl.Buffered(1))


def _norm_matmul_body(x_ref, g_ref, w_ref, o_ref, xn_ref):
    @pl.when(pl.program_id(1) == 0)
    def _():
        xn_ref[...] = _rms(x_ref[...], g_ref[...]).astype(BF16)

    o_ref[...] = jnp.dot(xn_ref[...], w_ref[...], preferred_element_type=F32)


def _norm_matmul(x, g, w, *, tm, tn):
    m, d = x.shape
    n = w.shape[1]
    assert m % tm == 0 and n % tn == 0
    return pl.pallas_call(
        _norm_matmul_body,
        grid=(m // tm, n // tn),
        in_specs=[
            pl.BlockSpec((tm, d), lambda i, j: (i, 0)),
            pl.BlockSpec((1, d), lambda i, j: (0, 0)),
            pl.BlockSpec((d, tn), lambda i, j: (0, j)),
        ],
        out_specs=pl.BlockSpec((tm, tn), lambda i, j: (i, j)),
        out_shape=jax.ShapeDtypeStruct((m, n), F32),
        scratch_shapes=[pltpu.VMEM((tm, d), BF16)],
        compiler_params=_params(("parallel", "arbitrary")),
        name="norm_matmul",
    )(x, g.reshape(1, d), w)


def _s5_discretize(lam_re, lam_im, log_dt, b_re, b_im, c_re, c_im):
    g, p = lam_re.shape
    gpb = LANE // SSM_GROUP
    nblk = g // gpb
    lr, li = lam_re.astype(F32), lam_im.astype(F32)
    dt = jnp.exp(log_dt.astype(F32))[:, None]
    mag = jnp.exp(lr * dt)
    a_re = mag * jnp.cos(li * dt)
    a_im = mag * jnp.sin(li * dt)
    den = lr * lr + li * li
    f_re = ((a_re - 1.0) * lr + a_im * li) / den
    f_im = (a_im * lr - (a_re - 1.0) * li) / den
    br, bi = b_re.astype(F32), b_im.astype(F32)
    bb_re = f_re[..., None] * br - f_im[..., None] * bi
    bb_im = f_re[..., None] * bi + f_im[..., None] * br
    eye = jnp.eye(gpb, dtype=F32)

    def b_blocks(bb):
        t = jnp.transpose(bb, (0, 2, 1)).reshape(nblk, gpb, SSM_GROUP, p)
        return jnp.einsum("jghp,gk->jghkp", t, eye).reshape(nblk, gpb * SSM_GROUP, gpb * p)

    def c_blocks(c):
        t = c.astype(F32).reshape(nblk, gpb, SSM_GROUP, p)
        return jnp.einsum("jghp,gk->jgpkh", t, eye).reshape(nblk, gpb * p, gpb * SSM_GROUP)

    bcat = jnp.concatenate([b_blocks(bb_re), b_blocks(bb_im)], axis=2).astype(BF16)
    ccat = jnp.concatenate([c_blocks(c_re), -c_blocks(c_im)], axis=1).astype(BF16)
    bro = lambda a: jnp.broadcast_to(a.reshape(1, g * p), (SUBLANE, g * p))
    return bro(a_re), bro(a_im), bcat, ccat


def _s5_body(u_ref, s0re_ref, s0im_ref, are_ref, aim_ref, bcat_ref, ccat_ref, d_ref,
             y_ref, ore_ref, oim_ref, sre, sim, st_re, st_im, *, tc, nblk, blk_state):
    c = pl.program_id(1)

    @pl.when(c == 0)
    def _():
        st_re[...] = s0re_ref[...]
        st_im[...] = s0im_ref[...]

    for j in range(nblk):
        ub = u_ref[:, j * LANE:(j + 1) * LANE].astype(BF16)
        bu = jnp.dot(ub, bcat_ref[j], preferred_element_type=F32)
        sre[:, j * blk_state:(j + 1) * blk_state] = bu[:, :blk_state]
        sim[:, j * blk_state:(j + 1) * blk_state] = bu[:, blk_state:]

    n_state = sre.shape[1]
    for strip in range(n_state // S5_STRIP):
        sl = slice(strip * S5_STRIP, (strip + 1) * S5_STRIP)
        ar = are_ref[:, sl]
        ai = aim_ref[:, sl]

        def step(t, carry, sl=sl, ar=ar, ai=ai):
            sr, si = carry
            rows = pl.ds(pl.multiple_of(t * SUBLANE, SUBLANE), SUBLANE)
            nr = ar * sr - ai * si + sre[rows, sl]
            ni = ar * si + ai * sr + sim[rows, sl]
            sre[rows, sl] = nr
            sim[rows, sl] = ni
            return nr, ni

        sr, si = lax.fori_loop(0, tc, step, (st_re[:, sl], st_im[:, sl]), unroll=4)
        st_re[:, sl] = sr
        st_im[:, sl] = si

    for j in range(nblk):
        cols = slice(j * blk_state, (j + 1) * blk_state)
        scat = jnp.concatenate([sre[:, cols].astype(BF16), sim[:, cols].astype(BF16)], axis=1)
        yj = jnp.dot(scat, ccat_ref[j], preferred_element_type=F32)
        lanes = slice(j * LANE, (j + 1) * LANE)
        yj = yj + d_ref[:, lanes] * u_ref[:, lanes]
        y_ref[:, lanes] = jax.nn.gelu(yj).astype(y_ref.dtype)

    @pl.when(c == pl.num_programs(1) - 1)
    def _():
        ore_ref[...] = st_re[...]
        oim_ref[...] = st_im[...]


def _s5(u, s0_re, s0_im, a_re, a_im, bcat, ccat, d_skip, *, n_seq, seq_len):
    w = u.shape[1]
    n_state = a_re.shape[1]
    nblk = bcat.shape[0]
    ng = n_seq // SEQ_PER_SCAN
    tc = min(seq_len, 32)
    rows = tc * SEQ_PER_SCAN
    u_tb = u.reshape(ng, SEQ_PER_SCAN, seq_len, w).transpose(0, 2, 1, 3).reshape(ng, seq_len * SEQ_PER_SCAN, w)
    s0_re = s0_re.astype(F32).reshape(ng, SEQ_PER_SCAN, n_state)
    s0_im = s0_im.astype(F32).reshape(ng, SEQ_PER_SCAN, n_state)
    state_spec = pl.BlockSpec((None, SEQ_PER_SCAN, n_state), lambda g, c: (g, 0, 0))
    y_tb, o_re, o_im = pl.pallas_call(
        functools.partial(_s5_body, tc=tc, nblk=nblk, blk_state=n_state // nblk),
        grid=(ng, seq_len // tc),
        in_specs=[
            pl.BlockSpec((None, rows, w), lambda g, c: (g, c, 0)),
            state_spec,
            state_spec,
            _resident(a_re.shape),
            _resident(a_im.shape),
            _resident(bcat.shape),
            _resident(ccat.shape),
            _resident((1, w)),
        ],
        out_specs=[
            pl.BlockSpec((None, rows, w), lambda g, c: (g, c, 0)),
            state_spec,
            state_spec,
        ],
        out_shape=[
            jax.ShapeDtypeStruct((ng, seq_len * SEQ_PER_SCAN, w), BF16),
            jax.ShapeDtypeStruct((ng, SEQ_PER_SCAN, n_state), F32),
            jax.ShapeDtypeStruct((ng, SEQ_PER_SCAN, n_state), F32),
        ],
        scratch_shapes=[
            pltpu.VMEM((rows, n_state), F32),
            pltpu.VMEM((rows, n_state), F32),
            pltpu.VMEM((SEQ_PER_SCAN, n_state), F32),
            pltpu.VMEM((SEQ_PER_SCAN, n_state), F32),
        ],
        compiler_params=_params(("parallel", "arbitrary")),
        name="s5_scan",
    )(u_tb, s0_re, s0_im, a_re, a_im, bcat, ccat, d_skip.astype(F32).reshape(1, w))
    y = y_tb.reshape(ng, seq_len, SEQ_PER_SCAN, w).transpose(0, 2, 1, 3).reshape(n_seq * seq_len, w)
    return y, o_re.reshape(n_seq, n_state), o_im.reshape(n_seq, n_state)


def _t5_bucket_table(n_q, n_k, n_prev):
    half = REL_BUCKETS // 2
    max_exact = half // 2
    assert (REL_MAX_DIST // max_exact) ** 2 == 2 ** (half - max_exact)
    rel = np.arange(n_k)[None, :] - n_prev - np.arange(n_q)[:, None]
    n = np.abs(rel)
    k = np.zeros_like(n)
    for kk in range(1, 2 * half):
        k = np.where(n * n >= max_exact * max_exact * (2 ** kk), kk, k)
    large = np.minimum(max_exact + k, half - 1)
    return ((rel > 0) * half + np.where(n < max_exact, n, large)).astype(np.int32)


def _swa_body(*refs, rows, window, n_heads, head_dim, has_past, has_prev_block):
    it = iter(refs)
    sink_ref = next(it)
    q_ref, kc_ref, vc_ref = next(it), next(it), next(it)
    kp_ref = vp_ref = kcache_ref = vcache_ref = None
    if has_prev_block:
        kp_ref, vp_ref = next(it), next(it)
    if has_past:
        kcache_ref, vcache_ref = next(it), next(it)
    bias_ref = next(it)
    o_ref = next(it)
    kx, vx = next(it), next(it)

    r = pl.program_id(1)
    n_k = window + CHUNK

    @pl.when(r == 0)
    def _():
        if has_past:
            kx[0:window, :] = kcache_ref[...].astype(BF16)
            vx[0:window, :] = vcache_ref[...].astype(BF16)
        else:
            kx[0:window, :] = jnp.zeros((window, kx.shape[1]), BF16)
            vx[0:window, :] = jnp.zeros((window, vx.shape[1]), BF16)

    if has_prev_block:
        @pl.when(r > 0)
        def _():
            kx[0:window, :] = kp_ref[...].astype(BF16)
            vx[0:window, :] = vp_ref[...].astype(BF16)

    kx[window:window + rows, :] = kc_ref[...].astype(BF16)
    vx[window:window + rows, :] = vc_ref[...].astype(BF16)

    scale = 1.0 / math.sqrt(head_dim)
    n_chunks = rows // CHUNK
    key_pos = lax.broadcasted_iota(jnp.int32, (CHUNK, n_k), 1)

    def chunk(cc, _):
        q0 = pl.multiple_of(cc * CHUNK, CHUNK)
        if not has_past:
            valid = key_pos + (r * n_chunks + cc) * CHUNK >= window
        for h in range(n_heads):
            kv = h // SWA_GROUP
            hs = slice(h * head_dim, (h + 1) * head_dim)
            ks = slice(kv * head_dim, (kv + 1) * head_dim)
            qh = (q_ref[pl.ds(q0, CHUNK), hs] * scale).astype(BF16)
            kb = kx[pl.ds(q0, n_k), ks]
            vb = vx[pl.ds(q0, n_k), ks]
            logits = lax.dot_general(qh, kb, (((1,), (1,)), ((), ())), preferred_element_type=F32)
            logits = logits + bias_ref[h]
            if not has_past:
                logits = jnp.where(valid, logits, -1e30)
            sink = sink_ref[h]
            m = jnp.maximum(jnp.max(logits, axis=-1, keepdims=True), sink)
            p = jnp.exp(logits - m)
            probs = p / (jnp.sum(p, axis=-1, keepdims=True) + jnp.exp(sink - m))
            o_ref[pl.ds(q0, CHUNK), hs] = jnp.dot(
                probs.astype(BF16), vb, preferred_element_type=F32).astype(o_ref.dtype)
        return 0

    lax.fori_loop(0, n_chunks, chunk, 0)


def _swa(h, k_cache, v_cache, sinks, bias, *, n_seq, seq_len, q_col, k_col, v_col, q_width, kv_width, window):
    has_past = k_cache is not None
    rows = min(seq_len, 256)
    has_prev_block = seq_len > rows
    assert rows % CHUNK == 0 and (not has_prev_block or rows % window == 0)
    n_heads = sinks.shape[0]
    head_dim = q_width // n_heads
    blocks_per_seq = seq_len // rows
    qb, kb, vb = q_col // q_width, k_col // kv_width, v_col // kv_width
    assert q_col % q_width == 0 and k_col % kv_width == 0 and v_col % kv_width == 0

    in_specs = [
        pl.BlockSpec(memory_space=pltpu.SMEM),
        pl.BlockSpec((rows, q_width), lambda b, r: (b * blocks_per_seq + r, qb)),
        pl.BlockSpec((rows, kv_width), lambda b, r: (b * blocks_per_seq + r, kb)),
        pl.BlockSpec((rows, kv_width), lambda b, r: (b * blocks_per_seq + r, vb)),
    ]
    args = [sinks.astype(F32), h, h, h]
    if has_prev_block:
        wpb = rows // window
        prev = lambda col: pl.BlockSpec(
            (window, kv_width), lambda b, r: (jnp.maximum((b * blocks_per_seq + r) * wpb - 1, 0), col))
        in_specs += [prev(kb), prev(vb)]
        args += [h, h]
    if has_past:
        cache = pl.BlockSpec((None, window, kv_width), lambda b, r: (b, 0, 0))
        in_specs += [cache, cache]
        args += [k_cache.reshape(n_seq, window, kv_width), v_cache.reshape(n_seq, window, kv_width)]
    in_specs.append(_resident(bias.shape))
    args.append(bias)

    return pl.pallas_call(
        functools.partial(_swa_body, rows=rows, window=window, n_heads=n_heads, head_dim=head_dim,
                          has_past=has_past, has_prev_block=has_prev_block),
        grid=(n_seq, blocks_per_seq),
        in_specs=in_specs,
        out_specs=pl.BlockSpec((rows, q_width), lambda b, r: (b * blocks_per_seq + r, 0)),
        out_shape=jax.ShapeDtypeStruct((n_seq * seq_len, q_width), BF16),
        scratch_shapes=[pltpu.VMEM((window + rows, kv_width), BF16)] * 2,
        compiler_params=_params(("parallel", "arbitrary")),
        name="swa_attention",
    )(*args)


def _merge(x, y_ssm, attn, h, w_glu, w_o, w_out, *, gs_col, ga_col, tm):
    m, d = x.shape
    gcol = math.gcd(gs_col, ga_col, d)
    assert gcol % LANE == 0
    nsub = d // gcol

    def body(x_ref, y_ref, a_ref, *rest):
        gs = jnp.concatenate([r[...] for r in rest[:nsub]], axis=1)
        ga = jnp.concatenate([r[...] for r in rest[nsub:2 * nsub]], axis=1)
        wglu_ref, wo_ref, wout_ref, o_ref = rest[2 * nsub:]
        gl = jnp.dot(y_ref[...], wglu_ref[...], preferred_element_type=F32)
        ys = gl[:, :d] * jax.nn.sigmoid(gl[:, d:])
        ya = jnp.dot(a_ref[...], wo_ref[...], preferred_element_type=F32)
        merged = jax.nn.sigmoid(gs) * ys + jax.nn.sigmoid(ga) * ya
        o_ref[...] = x_ref[...] + jnp.dot(merged.astype(BF16), wout_ref[...], preferred_element_type=F32)

    gate_specs = [pl.BlockSpec((tm, gcol), lambda i, c=c: (i, c))
                  for base in (gs_col, ga_col) for c in range(base // gcol, base // gcol + nsub)]
    return pl.pallas_call(
        body,
        grid=(m // tm,),
        in_specs=[
            pl.BlockSpec((tm, d), lambda i: (i, 0)),
            pl.BlockSpec((tm, y_ssm.shape[1]), lambda i: (i, 0)),
            pl.BlockSpec((tm, attn.shape[1]), lambda i: (i, 0)),
            *gate_specs,
            _resident(w_glu.shape),
            _resident(w_o.shape),
            _resident(w_out.shape),
        ],
        out_specs=pl.BlockSpec((tm, d), lambda i: (i, 0)),
        out_shape=jax.ShapeDtypeStruct((m, d), F32),
        compiler_params=_params(("parallel",)),
        name="merge_out_proj",
    )(x, y_ssm, attn, *([h] * (2 * nsub)), w_glu, w_o, w_out)


def _xattn_body(x_ref, g_ref, mk_ref, mv_ref, wq_ref, wo_ref, o_ref, *, n_heads, head_dim):
    x = x_ref[...]
    q = jnp.dot(_rms(x, g_ref[...]).astype(BF16), wq_ref[...], preferred_element_type=F32).astype(BF16)
    scale = 1.0 / math.sqrt(head_dim)
    outs = []
    for hd in range(n_heads):
        hs = slice(hd * head_dim, (hd + 1) * head_dim)
        logits = lax.dot_general(q[:, hs], mk_ref[:, hs].astype(BF16), (((1,), (1,)), ((), ())),
                                 preferred_element_type=F32) * scale
        m = jnp.max(logits, axis=-1, keepdims=True)
        p = jnp.exp(logits - m)
        probs = p / jnp.sum(p, axis=-1, keepdims=True)
        outs.append(jnp.dot(probs.astype(BF16), mv_ref[:, hs].astype(BF16), preferred_element_type=F32))
    att = jnp.concatenate(outs, axis=1).astype(BF16)
    o_ref[...] = x + jnp.dot(att, wo_ref[...], preferred_element_type=F32)


def _xattn(x, g, mk, mv, w_q, w_o, *, n_seq, seq_len, n_heads):
    m, d = x.shape
    tm = min(seq_len, 256)
    blocks_per_seq = seq_len // tm
    mem_len, xw = mk.shape[1], mk.shape[2]
    return pl.pallas_call(
        functools.partial(_xattn_body, n_heads=n_heads, head_dim=xw // n_heads),
        grid=(n_seq, blocks_per_seq),
        in_specs=[
            pl.BlockSpec((tm, d), lambda b, r: (b * blocks_per_seq + r, 0)),
            _resident((1, d)),
            pl.BlockSpec((None, mem_len, xw), lambda b, r: (b, 0, 0)),
            pl.BlockSpec((None, mem_len, xw), lambda b, r: (b, 0, 0)),
            _resident(w_q.shape),
            _resident(w_o.shape),
        ],
        out_specs=pl.BlockSpec((tm, d), lambda b, r: (b * blocks_per_seq + r, 0)),
        out_shape=jax.ShapeDtypeStruct((m, d), F32),
        compiler_params=_params(("parallel", "arbitrary")),
        name="cross_attention",
    )(x, g.reshape(1, d), mk, mv, w_q, w_o)


def _router_body(xp_ref, xs_ref, g_ref, wr_ref, br_ref, xn_ref, idx_ref, gate_ref, rank_ref, cnt_ref, carry,
                 *, n_exp, n_prompt_tiles):
    i = pl.program_id(0)

    @pl.when(i == 0)
    def _():
        carry[...] = jnp.zeros_like(carry)

    xn = _rms(jnp.where(i < n_prompt_tiles, xp_ref[...], xs_ref[...]), g_ref[...])
    xn_ref[...] = xn
    tm = xn.shape[0]
    logits = lax.dot_general(wr_ref[...], xn, (((1,), (1,)), ((), ())),
                             precision=lax.Precision.HIGHEST, preferred_element_type=F32) + br_ref[...]
    eid = lax.broadcasted_iota(jnp.int32, (n_exp, tm), 0)
    vals, hots = [], []
    work = logits
    for k in range(TOP_K):
        m = jnp.max(work, axis=0, keepdims=True)
        sel = jnp.min(jnp.where(work == m, eid, n_exp), axis=0, keepdims=True)
        hot = eid == sel
        vals.append(m)
        hots.append(hot)
        idx_ref[k:k + 1, :] = sel
        work = jnp.where(hot, -jnp.inf, work)
    ex = [jnp.exp(v - vals[0]) for v in vals]
    tot = ex[0] + ex[1] + ex[2] + ex[3]
    for k in range(TOP_K):
        gate_ref[k:k + 1, :] = ex[k] / tot

    chosen = (hots[0] | hots[1] | hots[2] | hots[3])
    a = jnp.where(chosen, 1.0, 0.0)
    earlier = (lax.broadcasted_iota(jnp.int32, (tm, tm), 0) < lax.broadcasted_iota(jnp.int32, (tm, tm), 1))
    prefix = jnp.dot(a.astype(BF16), jnp.where(earlier, 1.0, 0.0).astype(BF16),
                     preferred_element_type=F32) + carry[...]
    for k in range(TOP_K):
        rank_ref[k:k + 1, :] = jnp.sum(jnp.where(hots[k], prefix, 0.0), axis=0, keepdims=True).astype(jnp.int32)
    carry[...] = carry[...] + jnp.sum(a, axis=1, keepdims=True)
    cnt_ref[...] = carry[...].astype(jnp.int32)


def _two_group_specs(tm, d, npt):
    return [pl.BlockSpec((tm, d), lambda i: (jnp.minimum(i, npt - 1), 0)),
            pl.BlockSpec((tm, d), lambda i: (jnp.maximum(i - npt, 0), 0))]


def _router(x_p, x_s, g, w_router, b_router):
    d = x_p.shape[1]
    t = x_p.shape[0] + x_s.shape[0]
    n_exp = w_router.shape[1]
    tm = ROUTER_TM
    assert x_p.shape[0] % tm == 0 and x_s.shape[0] % tm == 0
    npt = x_p.shape[0] // tm
    row = pl.BlockSpec((TOP_K, tm), lambda i: (0, i))
    return pl.pallas_call(
        functools.partial(_router_body, n_exp=n_exp, n_prompt_tiles=npt),
        grid=(t // tm,),
        in_specs=[
            *_two_group_specs(tm, d, npt),
            _resident((1, d)),
            _resident((n_exp, d)),
            _resident((n_exp, 1)),
        ],
        out_specs=[pl.BlockSpec((tm, d), lambda i: (i, 0)), row, row, row,
                   pl.BlockSpec((n_exp, 1), lambda i: (0, 0))],
        out_shape=[
            jax.ShapeDtypeStruct((t, d), F32),
            jax.ShapeDtypeStruct((TOP_K, t), jnp.int32),
            jax.ShapeDtypeStruct((TOP_K, t), F32),
            jax.ShapeDtypeStruct((TOP_K, t), jnp.int32),
            jax.ShapeDtypeStruct((n_exp, 1), jnp.int32),
        ],
        scratch_shapes=[pltpu.VMEM((n_exp, 1), F32)],
        compiler_params=_params(("arbitrary",)),
        name="moe_router",
    )(x_p, x_s, g.reshape(1, d), w_router.astype(F32).T, b_router.astype(F32).reshape(n_exp, 1))


def _dispatch_body(pos_ref, x_ref, buf_ref, xs_ref, sem):
    del buf_ref
    tm = x_ref.shape[0]

    def row_copy(t, k):
        return pltpu.make_async_copy(x_ref.at[pl.ds(t, 1), :], xs_ref.at[pl.ds(pos_ref[k, t], 1), :], sem)

    def issue(t, _):
        for k in range(TOP_K):
            row_copy(t, k).start()
        return 0

    def drain(t, _):
        for k in range(TOP_K):
            row_copy(t, k).wait()
        return 0

    lax.fori_loop(0, tm, issue, 0)
    lax.fori_loop(0, tm, drain, 0)


def _dispatch(xn, pos, n_rows):
    t, d = xn.shape
    tm = TOK_TM
    return pl.pallas_call(
        _dispatch_body,
        grid=(t // tm,),
        in_specs=[
            pl.BlockSpec((TOP_K, tm), lambda i: (0, i), memory_space=pltpu.SMEM),
            pl.BlockSpec((tm, d), lambda i: (i, 0)),
            pl.BlockSpec(memory_space=pl.ANY),
        ],
        out_specs=pl.BlockSpec(memory_space=pl.ANY),
        out_shape=jax.ShapeDtypeStruct((n_rows, d), xn.dtype),
        input_output_aliases={2: 0},
        scratch_shapes=[pltpu.SemaphoreType.DMA],
        compiler_params=_params(("arbitrary",)),
        name="moe_dispatch",
    )(pos, xn, jnp.zeros((n_rows, d), xn.dtype))


def _experts_body(te_ref, nv_ref, last_ref, x_ref, wg_ref, wl_ref, bg_ref, bl_ref, wd_ref, bd_ref,
                  o_ref, acc, wgb, wlb, wdb, *, n_j):
    del te_ref, last_ref
    i, j = pl.program_id(0), pl.program_id(1)
    n_valid = nv_ref[i]
    tm, d = acc.shape

    @pl.when(n_valid > 0)
    def _():
        wgb[...] = wg_ref[...].astype(BF16)
        wlb[...] = wl_ref[...].astype(BF16)
        wdb[...] = wd_ref[...].astype(BF16)

        @pl.when(j == 0)
        def _():
            acc[...] = jnp.broadcast_to(bd_ref[...], (tm, d))

        for s in range(tm // MOE_SB):
            @pl.when(s * MOE_SB < n_valid)
            def _(s=s):
                rows = slice(s * MOE_SB, (s + 1) * MOE_SB)
                xs = x_ref[rows, :].astype(BF16)
                hg = jnp.dot(xs, wgb[...], preferred_element_type=F32) + bg_ref[...]
                hl = jnp.dot(xs, wlb[...], preferred_element_type=F32) + bl_ref[...]
                glu = jnp.minimum(hg, SWIGLU_LIMIT)
                lin = jnp.clip(hl, -SWIGLU_LIMIT, SWIGLU_LIMIT)
                act = glu * jax.nn.sigmoid(SWIGLU_ALPHA * glu) * (lin + 1.0)
                acc[rows, :] += jnp.dot(act.astype(BF16), wdb[...], preferred_element_type=F32)

        @pl.when(j == n_j - 1)
        def _():
            o_ref[...] = acc[...]


def _experts(xs, tile_expert, tile_valid, last_tile, w_up, b_up, w_down, b_down):
    n_rows, d = xs.shape
    depth, n_exp, _, ff2 = w_up.shape
    assert depth == 1
    d_ff = ff2 // 2
    n_tiles = n_rows // MOE_TM
    n_j = d_ff // MOE_TF
    b_up = b_up.reshape(depth, n_exp, 1, ff2)
    b_down = b_down.reshape(depth, n_exp, 1, d)

    def jsel(i, j, nv):
        return jnp.where(nv[i] > 0, j, n_j - 1)

    row = lambda i, j, te, nv, last: (jnp.minimum(i, last[0]), 0)
    grid_spec = pltpu.PrefetchScalarGridSpec(
        num_scalar_prefetch=3,
        grid=(n_tiles, n_j),
        in_specs=[
            pl.BlockSpec((MOE_TM, d), row),
            pl.BlockSpec((None, None, d, MOE_TF), lambda i, j, te, nv, last: (0, te[i], 0, jsel(i, j, nv))),
            pl.BlockSpec((None, None, d, MOE_TF), lambda i, j, te, nv, last: (0, te[i], 0, n_j + jsel(i, j, nv))),
            pl.BlockSpec((None, None, 1, MOE_TF), lambda i, j, te, nv, last: (0, te[i], 0, jsel(i, j, nv))),
            pl.BlockSpec((None, None, 1, MOE_TF), lambda i, j, te, nv, last: (0, te[i], 0, n_j + jsel(i, j, nv))),
            pl.BlockSpec((None, None, MOE_TF, d), lambda i, j, te, nv, last: (0, te[i], jsel(i, j, nv), 0)),
            pl.BlockSpec((None, None, 1, d), lambda i, j, te, nv, last: (0, te[i], 0, 0)),
        ],
        out_specs=pl.BlockSpec((MOE_TM, d), row),
        scratch_shapes=[
            pltpu.VMEM((MOE_TM, d), F32),
            pltpu.VMEM((d, MOE_TF), BF16),
            pltpu.VMEM((d, MOE_TF), BF16),
            pltpu.VMEM((MOE_TF, d), BF16),
        ],
    )
    return pl.pallas_call(
        functools.partial(_experts_body, n_j=n_j),
        grid_spec=grid_spec,
        out_shape=jax.ShapeDtypeStruct((n_rows, d), F32),
        compiler_params=_params(("arbitrary", "arbitrary")),
        name="moe_experts",
    )(tile_expert, tile_valid, last_tile, xs, w_up, w_up, b_up, b_up, w_down, b_down)


def _combine_body(pos_ref, xp_ref, xs_ref, gate_ref, g_ref, ys_ref, op_ref, os_ref, buf, sem, *, n_prompt_tiles):
    i = pl.program_id(0)
    tm = xp_ref.shape[0]

    def row_copy(t, k):
        return pltpu.make_async_copy(ys_ref.at[pl.ds(pos_ref[k, t], 1), :], buf.at[k, pl.ds(t, 1), :], sem)

    def issue(t, _):
        for k in range(TOP_K):
            row_copy(t, k).start()
        return 0

    def drain(t, _):
        for k in range(TOP_K):
            row_copy(t, k).wait()
        return 0

    lax.fori_loop(0, tm, issue, 0)
    lax.fori_loop(0, tm, drain, 0)

    out = jnp.where(i < n_prompt_tiles, xp_ref[...], xs_ref[...])
    for k in range(TOP_K):
        out = out + gate_ref[:, k:k + 1] * buf[k]
    y = _rms(out, g_ref[...])

    @pl.when(i < n_prompt_tiles)
    def _():
        op_ref[...] = y

    @pl.when(i >= n_prompt_tiles)
    def _():
        os_ref[...] = y


def _combine(x_p, x_s, ys, pos, gate_t, g_final):
    n_prompt, d = x_p.shape
    t = n_prompt + x_s.shape[0]
    tm = TOK_TM
    assert n_prompt % tm == 0 and t % tm == 0
    npt = n_prompt // tm
    return pl.pallas_call(
        functools.partial(_combine_body, n_prompt_tiles=npt),
        grid=(t // tm,),
        in_specs=[
            pl.BlockSpec((TOP_K, tm), lambda i: (0, i), memory_space=pltpu.SMEM),
            *_two_group_specs(tm, d, npt),
            pl.BlockSpec((tm, TOP_K), lambda i: (i, 0)),
            _resident((1, d)),
            pl.BlockSpec(memory_space=pl.ANY),
        ],
        out_specs=_two_group_specs(tm, d, npt),
        out_shape=[
            jax.ShapeDtypeStruct((n_prompt, d), F32),
            jax.ShapeDtypeStruct((t - n_prompt, d), F32),
        ],
        scratch_shapes=[pltpu.VMEM((TOP_K, tm, d), F32), pltpu.SemaphoreType.DMA],
        compiler_params=_params(("arbitrary",)),
        name="moe_combine",
    )(pos, x_p, x_s, gate_t, g_final.reshape(1, d), ys)


def _moe_layout(idx, rank, counts, n_tiles):
    n_exp = counts.shape[0]
    tiles_per_exp = (counts + MOE_TM - 1) // MOE_TM
    tile_end = jnp.cumsum(tiles_per_exp)
    tile_start = tile_end - tiles_per_exp
    n_used = tile_end[-1]
    tiles = jnp.arange(n_tiles, dtype=jnp.int32)
    te = jnp.minimum(jnp.searchsorted(tile_end, tiles, side="right"), n_exp - 1).astype(jnp.int32)
    te = jnp.where(tiles < n_used, te, te[jnp.maximum(n_used - 1, 0)])
    valid = jnp.clip(counts[te] - (tiles - tile_start[te]) * MOE_TM, 0, MOE_TM)
    valid = jnp.where(tiles < n_used, valid, 0).astype(jnp.int32)
    pos = (tile_start * MOE_TM)[idx] + rank
    last = jnp.maximum(n_used - 1, 0).astype(jnp.int32).reshape(1)
    return pos.astype(jnp.int32), te, valid, last


def kernel(x_prompt, x_sample, state_ssm_re, state_ssm_im, cache_swa_k, cache_swa_v, cache_mem_k, cache_mem_v, mem_prompt, w_in, ssm_lam_re, ssm_lam_im, ssm_log_dt, ssm_b_re, ssm_b_im, ssm_c_re, ssm_c_im, ssm_d, w_glu, swa_sinks, rel_bias, w_o_attn, w_out, w_xq, w_mk, w_mv, w_xo, w_router, b_router, w_up, b_up, w_down, b_down, norm_mix, norm_x, norm_mem, norm_moe, norm_final):
    depth = w_in.shape[0]
    assert depth == 1, "single-layer step"
    bp, lp, d = x_prompt.shape
    bs, ls, _ = x_sample.shape
    n_p, n_s = bp * lp, bs * ls
    t = n_p + n_s
    _, _, window, kvh, hd = cache_swa_k.shape
    kv_width = kvh * hd
    n_heads = swa_sinks.shape[1]
    q_width = n_heads * hd
    ssm_width = ssm_d.shape[1]
    _, _, mem_len, x_heads, x_hd = cache_mem_k.shape
    xw = x_heads * x_hd
    ssm_groups, ssm_state = ssm_lam_re.shape[1], ssm_lam_re.shape[2]
    q_col = ssm_width
    k_col = q_col + q_width
    v_col = k_col + kv_width
    gs_col = v_col + kv_width
    ga_col = gs_col + d

    l = 0
    w_in_b = w_in[l].astype(BF16)
    w_glu_b, w_o_b, w_out_b = w_glu[l].astype(BF16), w_o_attn[l].astype(BF16), w_out[l].astype(BF16)
    w_xq_b, w_xo_b = w_xq[l].astype(BF16), w_xo[l].astype(BF16)
    w_mkv_b = jnp.concatenate([w_mk[l], w_mv[l]], axis=1).astype(BF16)
    a_re, a_im, bcat, ccat = _s5_discretize(ssm_lam_re[l], ssm_lam_im[l], ssm_log_dt[l], ssm_b_re[l],
                                            ssm_b_im[l], ssm_c_re[l], ssm_c_im[l])
    bucket = _t5_bucket_table(CHUNK, window + CHUNK, window)
    bias = jnp.transpose(rel_bias.astype(F32)[bucket], (2, 0, 1))

    def mixers(x, n_seq, seq_len, s0_re, s0_im, k_cache, v_cache):
        xf = x.reshape(n_seq * seq_len, d)
        h = _norm_matmul(xf, norm_mix[l], w_in_b, tm=512, tn=512)
        y_ssm, s_re, s_im = _s5(h[:, :ssm_width], s0_re, s0_im, a_re, a_im, bcat, ccat, ssm_d[l],
                                n_seq=n_seq, seq_len=seq_len)
        attn = _swa(h, k_cache, v_cache, swa_sinks[l], bias, n_seq=n_seq, seq_len=seq_len, q_col=q_col,
                    k_col=k_col, v_col=v_col, q_width=q_width, kv_width=kv_width, window=window)
        x1 = _merge(xf, y_ssm, attn, h, w_glu_b, w_o_b, w_out_b, gs_col=gs_col, ga_col=ga_col, tm=256)
        h3 = h.reshape(n_seq, seq_len, -1)
        k_new = h3[:, :, k_col:k_col + kv_width]
        v_new = h3[:, :, v_col:v_col + kv_width]
        if k_cache is not None:
            k_new = jnp.concatenate([k_cache.reshape(n_seq, window, kv_width), k_new], axis=1)
            v_new = jnp.concatenate([v_cache.reshape(n_seq, window, kv_width), v_new], axis=1)
        k_new = k_new[:, -window:].reshape(1, n_seq, window, kvh, hd)
        v_new = v_new[:, -window:].reshape(1, n_seq, window, kvh, hd)
        state_shape = (1, n_seq, ssm_groups, ssm_state)
        return x1, s_re.reshape(state_shape), s_im.reshape(state_shape), k_new, v_new

    zero_state = jnp.zeros((bp, ssm_groups * ssm_state), F32)
    x1_p, sre_p, sim_p, k_p, v_p = mixers(x_prompt, bp, lp, zero_state, zero_state, None, None)
    x1_s, sre_s, sim_s, k_s, v_s = mixers(x_sample, bs, ls, state_ssm_re[l], state_ssm_im[l],
                                          cache_swa_k[l], cache_swa_v[l])

    mkv = _norm_matmul(mem_prompt.reshape(bp * mem_len, d), norm_mem[l], w_mkv_b, tm=512, tn=512)
    mkv = mkv.reshape(bp, mem_len, 2 * xw)
    mk_p, mv_p = mkv[:, :, :xw], mkv[:, :, xw:]

    x2_p = _xattn(x1_p, norm_x[l], mk_p, mv_p, w_xq_b, w_xo_b, n_seq=bp, seq_len=lp, n_heads=x_heads)
    x2_s = _xattn(x1_s, norm_x[l], cache_mem_k[l].reshape(bs, mem_len, xw), cache_mem_v[l].reshape(bs, mem_len, xw),
                  w_xq_b, w_xo_b, n_seq=bs, seq_len=ls, n_heads=x_heads)

    n_exp = w_router.shape[2]
    xn, idx, gate, rank, counts = _router(x2_p, x2_s, norm_moe[l], w_router[l], b_router[l])
    n_tiles = (t * TOP_K) // MOE_TM + n_exp
    pos, tile_expert, tile_valid, last_tile = _moe_layout(idx, rank, counts[:, 0], n_tiles)
    xs = _dispatch(xn, pos, n_tiles * MOE_TM)
    ys = _experts(xs, tile_expert, tile_valid, last_tile, w_up, b_up, w_down, b_down)
    y_p, y_s = _combine(x2_p, x2_s, ys, pos, gate.T, norm_final)

    return (y_p.reshape(bp, lp, d), y_s.reshape(bs, ls, d), sre_p, sim_p, k_p, v_p,
            mk_p.reshape(1, bp, mem_len, x_heads, x_hd), mv_p.reshape(1, bp, mem_len, x_heads, x_hd),
            sre_s, sim_s, k_s, v_s)
```

```python
import functools
import math

import numpy as np
import jax
import jax.numpy as jnp
from jax import lax
from jax.experimental import pallas as pl
from jax.experimental.pallas import tpu as pltpu

F32 = jnp.float32
BF16 = jnp.bfloat16

LANE = 128
SUBLANE = 8
VMEM_LIMIT = 56 * 1024 * 1024

CHUNK = 64
SSM_GROUP = 16
SWA_GROUP = 4
REL_BUCKETS = 32
REL_MAX_DIST = 128
TOP_K = 4
SWIGLU_LIMIT = 7.0
SWIGLU_ALPHA = 1.702
RMS_EPS = 1e-5
SEQ_PER_SCAN = SUBLANE

S5_STRIP = 512
MOE_TM = 1024
MOE_SB = 256
MOE_TF = 256
TOK_TM = 256
ROUTER_TM = 512


def _params(sem, vmem=VMEM_LIMIT):
    return pltpu.CompilerParams(dimension_semantics=sem, vmem_limit_bytes=vmem)


def _rms(x, g):
    return x * lax.rsqrt(jnp.mean(x * x, axis=-1, keepdims=True) + RMS_EPS) * g


def _resident(shape):
    nd = len(shape)
    return pl.BlockSpec(shape, lambda *_: (0,) * nd, pipeline_mode=pl.Buffered(1))


def _norm_matmul_body(x_ref, g_ref, w_ref, o_ref, xn_ref):
    @pl.when(pl.program_id(1) == 0)
    def _():
        xn_ref[...] = _rms(x_ref[...], g_ref[...]).astype(BF16)

    o_ref[...] = jnp.dot(xn_ref[...], w_ref[...], preferred_element_type=F32)


def _norm_matmul(x, g, w, *, tm, tn):
    m, d = x.shape
    n = w.shape[1]
    tm = min(tm, m)
    assert m % tm == 0 and n % tn == 0
    return pl.pallas_call(
        _norm_matmul_body,
        grid=(m // tm, n // tn),
        in_specs=[
            pl.BlockSpec((tm, d), lambda i, j: (i, 0)),
            pl.BlockSpec((1, d), lambda i, j: (0, 0)),
            pl.BlockSpec((d, tn), lambda i, j: (0, j)),
        ],
        out_specs=pl.BlockSpec((tm, tn), lambda i, j: (i, j)),
        out_shape=jax.ShapeDtypeStruct((m, n), F32),
        scratch_shapes=[pltpu.VMEM((tm, d), BF16)],
        compiler_params=_params(("parallel", "arbitrary")),
        name="norm_matmul",
    )(x, g.reshape(1, d), w)


def _s5_discretize(lam_re, lam_im, log_dt, b_re, b_im, c_re, c_im):
    g, p = lam_re.shape
    gpb = LANE // SSM_GROUP
    nblk = g // gpb
    lr, li = lam_re.astype(F32), lam_im.astype(F32)
    dt = jnp.exp(log_dt.astype(F32))[:, None]
    mag = jnp.exp(lr * dt)
    a_re = mag * jnp.cos(li * dt)
    a_im = mag * jnp.sin(li * dt)
    den = lr * lr + li * li
    f_re = ((a_re - 1.0) * lr + a_im * li) / den
    f_im = (a_im * lr - (a_re - 1.0) * li) / den
    br, bi = b_re.astype(F32), b_im.astype(F32)
    bb_re = f_re[..., None] * br - f_im[..., None] * bi
    bb_im = f_re[..., None] * bi + f_im[..., None] * br
    eye = jnp.eye(gpb, dtype=F32)

    def b_blocks(bb):
        t = jnp.transpose(bb, (0, 2, 1)).reshape(nblk, gpb, SSM_GROUP, p)
        return jnp.einsum("jghp,gk->jghkp", t, eye).reshape(nblk, gpb * SSM_GROUP, gpb * p)

    def c_blocks(c):
        t = c.astype(F32).reshape(nblk, gpb, SSM_GROUP, p)
        return jnp.einsum("jghp,gk->jgpkh", t, eye).reshape(nblk, gpb * p, gpb * SSM_GROUP)

    bcat = jnp.concatenate([b_blocks(bb_re), b_blocks(bb_im)], axis=2).astype(BF16)
    ccat = jnp.concatenate([c_blocks(c_re), -c_blocks(c_im)], axis=1).astype(BF16)
    bro = lambda a: jnp.broadcast_to(a.reshape(1, g * p), (SUBLANE, g * p))
    return bro(a_re), bro(a_im), bcat, ccat


def _s5_body(u_ref, s0re_ref, s0im_ref, are_ref, aim_ref, bcat_ref, ccat_ref, d_ref,
             y_ref, ore_ref, oim_ref, sre, sim, st_re, st_im, *, tc, nblk, blk_state):
    c = pl.program_id(1)

    @pl.when(c == 0)
    def _():
        st_re[...] = s0re_ref[...]
        st_im[...] = s0im_ref[...]

    for j in range(nblk):
        ub = u_ref[:, j * LANE:(j + 1) * LANE].astype(BF16)
        bu = jnp.dot(ub, bcat_ref[j], preferred_element_type=F32)
        sre[:, j * blk_state:(j + 1) * blk_state] = bu[:, :blk_state]
        sim[:, j * blk_state:(j + 1) * blk_state] = bu[:, blk_state:]

    n_state = sre.shape[1]
    for strip in range(n_state // S5_STRIP):
        sl = slice(strip * S5_STRIP, (strip + 1) * S5_STRIP)
        ar = are_ref[:, sl]
        ai = aim_ref[:, sl]

        def step(t, carry, sl=sl, ar=ar, ai=ai):
            sr, si = carry
            rows = pl.ds(pl.multiple_of(t * SUBLANE, SUBLANE), SUBLANE)
            nr = ar * sr - ai * si + sre[rows, sl]
            ni = ar * si + ai * sr + sim[rows, sl]
            sre[rows, sl] = nr
            sim[rows, sl] = ni
            return nr, ni

        sr, si = lax.fori_loop(0, tc, step, (st_re[:, sl], st_im[:, sl]), unroll=4)
        st_re[:, sl] = sr
        st_im[:, sl] = si

    for j in range(nblk):
        cols = slice(j * blk_state, (j + 1) * blk_state)
        scat = jnp.concatenate([sre[:, cols].astype(BF16), sim[:, cols].astype(BF16)], axis=1)
        yj = jnp.dot(scat, ccat_ref[j], preferred_element_type=F32)
        lanes = slice(j * LANE, (j + 1) * LANE)
        yj = yj + d_ref[:, lanes] * u_ref[:, lanes]
        y_ref[:, lanes] = jax.nn.gelu(yj).astype(y_ref.dtype)

    @pl.when(c == pl.num_programs(1) - 1)
    def _():
        ore_ref[...] = st_re[...]
        oim_ref[...] = st_im[...]


def _s5(u, s0_re, s0_im, a_re, a_im, bcat, ccat, d_skip, *, n_seq, seq_len):
    w = u.shape[1]
    n_state = a_re.shape[1]
    nblk = bcat.shape[0]
    ng = n_seq // SEQ_PER_SCAN
    tc = min(seq_len, 32)
    rows = tc * SEQ_PER_SCAN
    u_tb = u.reshape(ng, SEQ_PER_SCAN, seq_len, w).transpose(0, 2, 1, 3).reshape(ng, seq_len * SEQ_PER_SCAN, w)
    s0_re = s0_re.astype(F32).reshape(ng, SEQ_PER_SCAN, n_state)
    s0_im = s0_im.astype(F32).reshape(ng, SEQ_PER_SCAN, n_state)
    state_spec = pl.BlockSpec((None, SEQ_PER_SCAN, n_state), lambda g, c: (g, 0, 0))
    y_tb, o_re, o_im = pl.pallas_call(
        functools.partial(_s5_body, tc=tc, nblk=nblk, blk_state=n_state // nblk),
        grid=(ng, seq_len // tc),
        in_specs=[
            pl.BlockSpec((None, rows, w), lambda g, c: (g, c, 0)),
            state_spec,
            state_spec,
            _resident(a_re.shape),
            _resident(a_im.shape),
            _resident(bcat.shape),
            _resident(ccat.shape),
            _resident((1, w)),
        ],
        out_specs=[
            pl.BlockSpec((None, rows, w), lambda g, c: (g, c, 0)),
            state_spec,
            state_spec,
        ],
        out_shape=[
            jax.ShapeDtypeStruct((ng, seq_len * SEQ_PER_SCAN, w), BF16),
            jax.ShapeDtypeStruct((ng, SEQ_PER_SCAN, n_state), F32),
            jax.ShapeDtypeStruct((ng, SEQ_PER_SCAN, n_state), F32),
        ],
        scratch_shapes=[
            pltpu.VMEM((rows, n_state), F32),
            pltpu.VMEM((rows, n_state), F32),
            pltpu.VMEM((SEQ_PER_SCAN, n_state), F32),
            pltpu.VMEM((SEQ_PER_SCAN, n_state), F32),
        ],
        compiler_params=_params(("parallel", "arbitrary")),
        name="s5_scan",
    )(u_tb, s0_re, s0_im, a_re, a_im, bcat, ccat, d_skip.astype(F32).reshape(1, w))
    y = y_tb.reshape(ng, seq_len, SEQ_PER_SCAN, w).transpose(0, 2, 1, 3).reshape(n_seq * seq_len, w)
    return y, o_re.reshape(n_seq, n_state), o_im.reshape(n_seq, n_state)


def _t5_bucket_table(n_q, n_k, n_prev):
    half = REL_BUCKETS // 2
    max_exact = half // 2
    assert (REL_MAX_DIST // max_exact) ** 2 == 2 ** (half - max_exact)
    rel = np.arange(n_k)[None, :] - n_prev - np.arange(n_q)[:, None]
    n = np.abs(rel)
    k = np.zeros_like(n)
    for kk in range(1, 2 * half):
        k = np.where(n * n >= max_exact * max_exact * (2 ** kk), kk, k)
    large = np.minimum(max_exact + k, half - 1)
    return ((rel > 0) * half + np.where(n < max_exact, n, large)).astype(np.int32)


def _swa_body(*refs, rows, window, n_kv, head_dim, has_past, has_prev_block):
    it = iter(refs)
    q_ref, kc_ref, vc_ref = next(it), next(it), next(it)
    kp_ref = vp_ref = kcache_ref = vcache_ref = None
    if has_prev_block:
        kp_ref, vp_ref = next(it), next(it)
    if has_past:
        kcache_ref, vcache_ref = next(it), next(it)
    bias_ref, sink_ref = next(it), next(it)
    o_ref = next(it)
    kx, vx = next(it), next(it)

    r = pl.program_id(1)
    n_k = window + CHUNK
    grp = SWA_GROUP * head_dim

    def put(dst, lo, src):
        for kv in range(n_kv):
            dst[kv, lo:lo + src.shape[0], :] = src[:, kv * head_dim:(kv + 1) * head_dim].astype(BF16)

    @pl.when(r == 0)
    def _():
        if has_past:
            put(kx, 0, kcache_ref[...])
            put(vx, 0, vcache_ref[...])
        else:
            kx[:, 0:window, :] = jnp.zeros((n_kv, window, head_dim), BF16)
            vx[:, 0:window, :] = jnp.zeros((n_kv, window, head_dim), BF16)

    if has_prev_block:
        @pl.when(r > 0)
        def _():
            put(kx, 0, kp_ref[...])
            put(vx, 0, vp_ref[...])

    put(kx, window, kc_ref[...])
    put(vx, window, vc_ref[...])

    scale = 1.0 / math.sqrt(head_dim)
    n_chunks = rows // CHUNK
    key_pos = lax.broadcasted_iota(jnp.int32, (SWA_GROUP * CHUNK, n_k), 1)

    def chunk(cc, _):
        q0 = pl.multiple_of(cc * CHUNK, CHUNK)
        qrows = pl.ds(q0, CHUNK)
        krows = pl.ds(q0, n_k)
        if not has_past:
            valid = key_pos + (r * n_chunks + cc) * CHUNK >= window
        logits = []
        for kv in range(n_kv):
            qg = q_ref[qrows, kv * grp:(kv + 1) * grp] * scale
            qs = jnp.concatenate([qg[:, g * head_dim:(g + 1) * head_dim] for g in range(SWA_GROUP)], axis=0)
            lg = lax.dot_general(qs.astype(BF16), kx[kv, krows, :], (((1,), (1,)), ((), ())),
                                 preferred_element_type=F32) + bias_ref[kv]
            if not has_past:
                lg = jnp.where(valid, lg, -1e30)
            logits.append(lg)
        probs = []
        for kv in range(n_kv):
            sink = sink_ref[kv]
            m = jnp.maximum(jnp.max(logits[kv], axis=-1, keepdims=True), sink)
            p = jnp.exp(logits[kv] - m)
            den = jnp.sum(p, axis=-1, keepdims=True) + jnp.exp(sink - m)
            probs.append((p * (1.0 / den)).astype(BF16))
        for kv in range(n_kv):
            out = jnp.dot(probs[kv], vx[kv, krows, :], preferred_element_type=F32)
            og = jnp.concatenate([out[g * CHUNK:(g + 1) * CHUNK, :] for g in range(SWA_GROUP)], axis=1)
            o_ref[qrows, kv * grp:(kv + 1) * grp] = og.astype(o_ref.dtype)
        return 0

    lax.fori_loop(0, n_chunks, chunk, 0)


def _swa(h, k_cache, v_cache, sinks, bias, *, n_seq, seq_len, q_col, k_col, v_col, q_width, kv_width, window):
    has_past = k_cache is not None
    rows = min(seq_len, 256)
    has_prev_block = seq_len > rows
    assert rows % CHUNK == 0 and (not has_prev_block or rows % window == 0)
    n_heads = sinks.shape[0]
    head_dim = q_width // n_heads
    n_kv = n_heads // SWA_GROUP
    n_k = window + CHUNK
    blocks_per_seq = seq_len // rows
    qb, kb, vb = q_col // q_width, k_col // kv_width, v_col // kv_width
    assert q_col % q_width == 0 and k_col % kv_width == 0 and v_col % kv_width == 0
    bias_g = bias.reshape(n_kv, SWA_GROUP * CHUNK, n_k)
    sink_g = jnp.repeat(sinks.astype(F32).reshape(n_kv, SWA_GROUP), CHUNK, axis=1)[..., None]

    in_specs = [
        pl.BlockSpec((rows, q_width), lambda b, r: (b * blocks_per_seq + r, qb)),
        pl.BlockSpec((rows, kv_width), lambda b, r: (b * blocks_per_seq + r, kb)),
        pl.BlockSpec((rows, kv_width), lambda b, r: (b * blocks_per_seq + r, vb)),
    ]
    args = [h, h, h]
    if has_prev_block:
        wpb = rows // window
        prev = lambda col: pl.BlockSpec(
            (window, kv_width), lambda b, r: (jnp.maximum((b * blocks_per_seq + r) * wpb - 1, 0), col))
        in_specs += [prev(kb), prev(vb)]
        args += [h, h]
    if has_past:
        cache = pl.BlockSpec((None, window, kv_width), lambda b, r: (b, 0, 0))
        in_specs += [cache, cache]
        args += [k_cache.reshape(n_seq, window, kv_width), v_cache.reshape(n_seq, window, kv_width)]
    in_specs += [_resident(bias_g.shape), _resident(sink_g.shape)]
    args += [bias_g, sink_g]

    return pl.pallas_call(
        functools.partial(_swa_body, rows=rows, window=window, n_kv=n_kv, head_dim=head_dim,
                          has_past=has_past, has_prev_block=has_prev_block),
        grid=(n_seq, blocks_per_seq),
        in_specs=in_specs,
        out_specs=pl.BlockSpec((rows, q_width), lambda b, r: (b * blocks_per_seq + r, 0)),
        out_shape=jax.ShapeDtypeStruct((n_seq * seq_len, q_width), BF16),
        scratch_shapes=[pltpu.VMEM((n_kv, window + rows, head_dim), BF16)] * 2,
        compiler_params=_params(("parallel", "arbitrary")),
        name="swa_attention",
    )(*args)


def _merge(x, y_ssm, attn, h, w_glu, w_o, w_out, *, gs_col, ga_col, tm):
    m, d = x.shape
    gcol = math.gcd(gs_col, ga_col, d)
    assert gcol % LANE == 0
    nsub = d // gcol

    def body(x_ref, y_ref, a_ref, *rest):
        gs = jnp.concatenate([r[...] for r in rest[:nsub]], axis=1)
        ga = jnp.concatenate([r[...] for r in rest[nsub:2 * nsub]], axis=1)
        wglu_ref, wo_ref, wout_ref, o_ref = rest[2 * nsub:]
        gl = jnp.dot(y_ref[...], wglu_ref[...], preferred_element_type=F32)
        ys = gl[:, :d] * jax.nn.sigmoid(gl[:, d:])
        ya = jnp.dot(a_ref[...], wo_ref[...], preferred_element_type=F32)
        merged = jax.nn.sigmoid(gs) * ys + jax.nn.sigmoid(ga) * ya
        o_ref[...] = x_ref[...] + jnp.dot(merged.astype(BF16), wout_ref[...], preferred_element_type=F32)

    gate_specs = [pl.BlockSpec((tm, gcol), lambda i, c=c: (i, c))
                  for base in (gs_col, ga_col) for c in range(base // gcol, base // gcol + nsub)]
    return pl.pallas_call(
        body,
        grid=(m // tm,),
        in_specs=[
            pl.BlockSpec((tm, d), lambda i: (i, 0)),
            pl.BlockSpec((tm, y_ssm.shape[1]), lambda i: (i, 0)),
            pl.BlockSpec((tm, attn.shape[1]), lambda i: (i, 0)),
            *gate_specs,
            _resident(w_glu.shape),
            _resident(w_o.shape),
            _resident(w_out.shape),
        ],
        out_specs=pl.BlockSpec((tm, d), lambda i: (i, 0)),
        out_shape=jax.ShapeDtypeStruct((m, d), F32),
        compiler_params=_params(("parallel",)),
        name="merge_out_proj",
    )(x, y_ssm, attn, *([h] * (2 * nsub)), w_glu, w_o, w_out)


def _xattn_body(x_ref, g_ref, mk_ref, mv_ref, wq_ref, wo_ref, o_ref, *, n_heads, head_dim):
    x = x_ref[...]
    q = jnp.dot(_rms(x, g_ref[...]).astype(BF16), wq_ref[...], preferred_element_type=F32).astype(BF16)
    scale = 1.0 / math.sqrt(head_dim)
    outs = []
    for hd in range(n_heads):
        hs = slice(hd * head_dim, (hd + 1) * head_dim)
        logits = lax.dot_general(q[:, hs], mk_ref[:, hs].astype(BF16), (((1,), (1,)), ((), ())),
                                 preferred_element_type=F32) * scale
        m = jnp.max(logits, axis=-1, keepdims=True)
        p = jnp.exp(logits - m)
        probs = p / jnp.sum(p, axis=-1, keepdims=True)
        outs.append(jnp.dot(probs.astype(BF16), mv_ref[:, hs].astype(BF16), preferred_element_type=F32))
    att = jnp.concatenate(outs, axis=1).astype(BF16)
    o_ref[...] = x + jnp.dot(att, wo_ref[...], preferred_element_type=F32)


def _xattn(x, g, mk, mv, w_q, w_o, *, n_seq, seq_len, n_heads):
    m, d = x.shape
    tm = min(seq_len, 256)
    blocks_per_seq = seq_len // tm
    mem_len, xw = mk.shape[1], mk.shape[2]
    return pl.pallas_call(
        functools.partial(_xattn_body, n_heads=n_heads, head_dim=xw // n_heads),
        grid=(n_seq, blocks_per_seq),
        in_specs=[
            pl.BlockSpec((tm, d), lambda b, r: (b * blocks_per_seq + r, 0)),
            _resident((1, d)),
            pl.BlockSpec((None, mem_len, xw), lambda b, r: (b, 0, 0)),
            pl.BlockSpec((None, mem_len, xw), lambda b, r: (b, 0, 0)),
            _resident(w_q.shape),
            _resident(w_o.shape),
        ],
        out_specs=pl.BlockSpec((tm, d), lambda b, r: (b * blocks_per_seq + r, 0)),
        out_shape=jax.ShapeDtypeStruct((m, d), F32),
        compiler_params=_params(("parallel", "arbitrary")),
        name="cross_attention",
    )(x, g.reshape(1, d), mk, mv, w_q, w_o)


def _router_body(xp_ref, xs_ref, g_ref, wr_ref, br_ref, xn_ref, idx_ref, gate_ref, rank_ref, cnt_ref, carry,
                 *, n_exp, n_prompt_tiles):
    i = pl.program_id(0)

    @pl.when(i == 0)
    def _():
        carry[...] = jnp.zeros_like(carry)

    xn = _rms(jnp.where(i < n_prompt_tiles, xp_ref[...], xs_ref[...]), g_ref[...])
    xn_ref[...] = xn
    tm = xn.shape[0]
    logits = lax.dot_general(wr_ref[...], xn, (((1,), (1,)), ((), ())),
                             precision=lax.Precision.HIGHEST, preferred_element_type=F32) + br_ref[...]
    eid = lax.broadcasted_iota(jnp.int32, (n_exp, tm), 0)
    vals, hots = [], []
    work = logits
    for k in range(TOP_K):
        m = jnp.max(work, axis=0, keepdims=True)
        sel = jnp.min(jnp.where(work == m, eid, n_exp), axis=0, keepdims=True)
        hot = eid == sel
        vals.append(m)
        hots.append(hot)
        idx_ref[k:k + 1, :] = sel
        work = jnp.where(hot, -jnp.inf, work)
    ex = [jnp.exp(v - vals[0]) for v in vals]
    tot = ex[0] + ex[1] + ex[2] + ex[3]
    for k in range(TOP_K):
        gate_ref[k:k + 1, :] = ex[k] / tot

    chosen = (hots[0] | hots[1] | hots[2] | hots[3])
    a = jnp.where(chosen, 1.0, 0.0)
    earlier = (lax.broadcasted_iota(jnp.int32, (tm, tm), 0) < lax.broadcasted_iota(jnp.int32, (tm, tm), 1))
    prefix = jnp.dot(a.astype(BF16), jnp.where(earlier, 1.0, 0.0).astype(BF16),
                     preferred_element_type=F32) + carry[...]
    for k in range(TOP_K):
        rank_ref[k:k + 1, :] = jnp.sum(jnp.where(hots[k], prefix, 0.0), axis=0, keepdims=True).astype(jnp.int32)
    carry[...] = carry[...] + jnp.sum(a, axis=1, keepdims=True)
    cnt_ref[...] = carry[...].astype(jnp.int32)


def _two_group_specs(tm, d, npt):
    return [pl.BlockSpec((tm, d), lambda i: (jnp.minimum(i, npt - 1), 0)),
            pl.BlockSpec((tm, d), lambda i: (jnp.maximum(i - npt, 0), 0))]


def _router(x_p, x_s, g, w_router, b_router):
    d = x_p.shape[1]
    t = x_p.shape[0] + x_s.shape[0]
    n_exp = w_router.shape[1]
    tm = ROUTER_TM
    assert x_p.shape[0] % tm == 0 and x_s.shape[0] % tm == 0
    npt = x_p.shape[0] // tm
    row = pl.BlockSpec((TOP_K, tm), lambda i: (0, i))
    return pl.pallas_call(
        functools.partial(_router_body, n_exp=n_exp, n_prompt_tiles=npt),
        grid=(t // tm,),
        in_specs=[
            *_two_group_specs(tm, d, npt),
            _resident((1, d)),
            _resident((n_exp, d)),
            _resident((n_exp, 1)),
        ],
        out_specs=[pl.BlockSpec((tm, d), lambda i: (i, 0)), row, row, row,
                   pl.BlockSpec((n_exp, 1), lambda i: (0, 0))],
        out_shape=[
            jax.ShapeDtypeStruct((t, d), F32),
            jax.ShapeDtypeStruct((TOP_K, t), jnp.int32),
            jax.ShapeDtypeStruct((TOP_K, t), F32),
            jax.ShapeDtypeStruct((TOP_K, t), jnp.int32),
            jax.ShapeDtypeStruct((n_exp, 1), jnp.int32),
        ],
        scratch_shapes=[pltpu.VMEM((n_exp, 1), F32)],
        compiler_params=_params(("arbitrary",)),
        name="moe_router",
    )(x_p, x_s, g.reshape(1, d), w_router.astype(F32).T, b_router.astype(F32).reshape(n_exp, 1))


def _dispatch_body(pos_ref, fill_ref, x_ref, xs_ref, zeros, sem, fill_sem, *, n_exp):
    tm, d = x_ref.shape
    fill_rows = zeros.shape[0]

    def fill_copy(e):
        start = pl.multiple_of(fill_ref[e], SUBLANE)
        return pltpu.make_async_copy(zeros, xs_ref.at[pl.ds(start, fill_rows), :], fill_sem)

    @pl.when(pl.program_id(0) == 0)
    def _():
        zeros[...] = jnp.zeros_like(zeros)
        for e in range(n_exp):
            fill_copy(e).start()
        for e in range(n_exp):
            fill_copy(e).wait()

    def issue(t, _):
        for k in range(TOP_K):
            pltpu.make_async_copy(x_ref.at[pl.ds(t, 1), :], xs_ref.at[pl.ds(pos_ref[k, t], 1), :], sem).start()
        return 0

    lax.fori_loop(0, tm, issue, 0, unroll=8)
    for k in range(TOP_K):
        pltpu.make_async_copy(x_ref, xs_ref.at[pl.ds(0, tm), :], sem).wait()


def _dispatch(xn, pos, fill_start, n_rows):
    t, d = xn.shape
    tm = TOK_TM
    n_exp = fill_start.shape[0]
    fill_rows = MOE_SB + SUBLANE
    return pl.pallas_call(
        functools.partial(_dispatch_body, n_exp=n_exp),
        grid=(t // tm,),
        in_specs=[
            pl.BlockSpec((TOP_K, tm), lambda i: (0, i), memory_space=pltpu.SMEM),
            pl.BlockSpec(memory_space=pltpu.SMEM),
            pl.BlockSpec((tm, d), lambda i: (i, 0)),
        ],
        out_specs=pl.BlockSpec(memory_space=pl.ANY),
        out_shape=jax.ShapeDtypeStruct((n_rows + fill_rows, d), xn.dtype),
        scratch_shapes=[pltpu.VMEM((fill_rows, d), xn.dtype), pltpu.SemaphoreType.DMA, pltpu.SemaphoreType.DMA],
        compiler_params=_params(("arbitrary",)),
        name="moe_dispatch",
    )(pos, fill_start, xn)


def _experts_body(te_ref, nv_ref, last_ref, x_ref, wg_ref, wl_ref, bg_ref, bl_ref, wd_ref, bd_ref,
                  o_ref, xb, wgb, wlb, wdb, *, n_j):
    del te_ref, last_ref, n_j
    i, j = pl.program_id(0), pl.program_id(1)
    n_valid = nv_ref[i]
    tm, d = o_ref.shape
    n_sub = tm // MOE_SB
    rows_of = lambda s: slice(s * MOE_SB, (s + 1) * MOE_SB)

    def compute(s, wg, wl, wd):
        xs = xb[rows_of(s), :]
        hg = jnp.dot(xs, wg, preferred_element_type=F32) + bg_ref[...]
        hl = jnp.dot(xs, wl, preferred_element_type=F32) + bl_ref[...]
        glu = jnp.minimum(hg, SWIGLU_LIMIT)
        lin = jnp.clip(hl, -SWIGLU_LIMIT, SWIGLU_LIMIT)
        act = glu * jax.nn.sigmoid(SWIGLU_ALPHA * glu) * (lin + 1.0)
        o_ref[rows_of(s), :] += jnp.dot(act.astype(BF16), wd, preferred_element_type=F32)

    @pl.when(j == 0)
    def _():
        for s in range(n_sub):
            @pl.when(s * MOE_SB < n_valid)
            def _(s=s):
                xb[rows_of(s), :] = x_ref[rows_of(s), :].astype(BF16)
                o_ref[rows_of(s), :] = jnp.broadcast_to(bd_ref[...], (MOE_SB, d))

            @pl.when(s * MOE_SB >= n_valid)
            def _(s=s):
                o_ref[rows_of(s), :] = jnp.zeros((MOE_SB, d), F32)

    @pl.when(n_valid > 0)
    def _():
        wg = wg_ref[...].astype(BF16)
        wl = wl_ref[...].astype(BF16)
        wd = wd_ref[...].astype(BF16)
        wgb[...] = wg
        wlb[...] = wl
        wdb[...] = wd
        compute(0, wg, wl, wd)
        for s in range(1, n_sub):
            @pl.when(s * MOE_SB < n_valid)
            def _(s=s):
                compute(s, wgb[...], wlb[...], wdb[...])


def _experts(xs, tile_expert, tile_valid, last_tile, w_up, b_up, w_down, b_down):
    d = xs.shape[1]
    n_tiles = tile_expert.shape[0]
    n_rows = n_tiles * MOE_TM
    depth, n_exp, _, ff2 = w_up.shape
    assert depth == 1
    d_ff = ff2 // 2
    n_j = d_ff // MOE_TF
    b_up = b_up.reshape(depth, n_exp, 1, ff2)
    b_down = b_down.reshape(depth, n_exp, 1, d)

    def jsel(i, j, nv):
        return jnp.where(nv[i] > 0, j, n_j - 1)

    row = lambda i, j, te, nv, last: (jnp.minimum(i, last[0]), 0)
    out_row = lambda i, j, te, nv, last: (i, 0)
    grid_spec = pltpu.PrefetchScalarGridSpec(
        num_scalar_prefetch=3,
        grid=(n_tiles, n_j),
        in_specs=[
            pl.BlockSpec((MOE_TM, d), row),
            pl.BlockSpec((None, None, d, MOE_TF), lambda i, j, te, nv, last: (0, te[i], 0, jsel(i, j, nv))),
            pl.BlockSpec((None, None, d, MOE_TF), lambda i, j, te, nv, last: (0, te[i], 0, n_j + jsel(i, j, nv))),
            pl.BlockSpec((None, None, 1, MOE_TF), lambda i, j, te, nv, last: (0, te[i], 0, jsel(i, j, nv))),
            pl.BlockSpec((None, None, 1, MOE_TF), lambda i, j, te, nv, last: (0, te[i], 0, n_j + jsel(i, j, nv))),
            pl.BlockSpec((None, None, MOE_TF, d), lambda i, j, te, nv, last: (0, te[i], jsel(i, j, nv), 0)),
            pl.BlockSpec((None, None, 1, d), lambda i, j, te, nv, last: (0, te[i], 0, 0)),
        ],
        out_specs=pl.BlockSpec((MOE_TM, d), out_row),
        scratch_shapes=[
            pltpu.VMEM((MOE_TM, d), BF16),
            pltpu.VMEM((d, MOE_TF), BF16),
            pltpu.VMEM((d, MOE_TF), BF16),
            pltpu.VMEM((MOE_TF, d), BF16),
        ],
    )
    return pl.pallas_call(
        functools.partial(_experts_body, n_j=n_j),
        grid_spec=grid_spec,
        out_shape=jax.ShapeDtypeStruct((n_rows, d), F32),
        compiler_params=_params(("arbitrary", "arbitrary")),
        name="moe_experts",
    )(tile_expert, tile_valid, last_tile, xs, w_up, w_up, b_up, b_up, w_down, b_down)


def _combine_body(pos_ref, xp_ref, xs_ref, gate_ref, g_ref, ys_ref, op_ref, os_ref, buf, sem, *, n_prompt_tiles):
    i = pl.program_id(0)
    tm = xp_ref.shape[0]

    def issue(t, _):
        for k in range(TOP_K):
            pltpu.make_async_copy(ys_ref.at[pl.ds(pos_ref[k, t], 1), :], buf.at[k, pl.ds(t, 1), :], sem).start()
        return 0

    lax.fori_loop(0, tm, issue, 0, unroll=8)
    for k in range(TOP_K):
        pltpu.make_async_copy(ys_ref.at[pl.ds(0, tm), :], buf.at[k], sem).wait()

    out = jnp.where(i < n_prompt_tiles, xp_ref[...], xs_ref[...])
    for k in range(TOP_K):
        out = out + gate_ref[:, k:k + 1] * buf[k]
    y = _rms(out, g_ref[...])

    @pl.when(i < n_prompt_tiles)
    def _():
        op_ref[...] = y

    @pl.when(i >= n_prompt_tiles)
    def _():
        os_ref[...] = y


def _combine(x_p, x_s, ys, pos, gate_t, g_final):
    n_prompt, d = x_p.shape
    t = n_prompt + x_s.shape[0]
    tm = TOK_TM
    assert n_prompt % tm == 0 and t % tm == 0
    npt = n_prompt // tm
    return pl.pallas_call(
        functools.partial(_combine_body, n_prompt_tiles=npt),
        grid=(t // tm,),
        in_specs=[
            pl.BlockSpec((TOP_K, tm), lambda i: (0, i), memory_space=pltpu.SMEM),
            *_two_group_specs(tm, d, npt),
            pl.BlockSpec((tm, TOP_K), lambda i: (i, 0)),
            _resident((1, d)),
            pl.BlockSpec(memory_space=pl.ANY),
        ],
        out_specs=_two_group_specs(tm, d, npt),
        out_shape=[
            jax.ShapeDtypeStruct((n_prompt, d), F32),
            jax.ShapeDtypeStruct((t - n_prompt, d), F32),
        ],
        scratch_shapes=[pltpu.VMEM((TOP_K, tm, d), F32), pltpu.SemaphoreType.DMA],
        compiler_params=_params(("arbitrary",)),
        name="moe_combine",
    )(pos, x_p, x_s, gate_t, g_final.reshape(1, d), ys)


def _moe_layout(idx, rank, counts, n_tiles):
    n_exp = counts.shape[0]
    tiles_per_exp = (counts + MOE_TM - 1) // MOE_TM
    tile_end = jnp.cumsum(tiles_per_exp)
    tile_start = tile_end - tiles_per_exp
    n_used = tile_end[-1]
    tiles = jnp.arange(n_tiles, dtype=jnp.int32)
    te = jnp.minimum(jnp.searchsorted(tile_end, tiles, side="right"), n_exp - 1).astype(jnp.int32)
    te = jnp.where(tiles < n_used, te, te[jnp.maximum(n_used - 1, 0)])
    valid = jnp.clip(counts[te] - (tiles - tile_start[te]) * MOE_TM, 0, MOE_TM)
    valid = jnp.where(tiles < n_used, valid, 0).astype(jnp.int32)
    row_start = tile_start * MOE_TM
    hot = idx[..., None] == jnp.arange(n_exp, dtype=jnp.int32)
    pos = rank + jnp.sum(jnp.where(hot, row_start, 0), axis=-1)
    last = jnp.maximum(n_used - 1, 0).astype(jnp.int32).reshape(1)
    fill_start = row_start + (counts // SUBLANE) * SUBLANE
    return pos.astype(jnp.int32), te, valid, last, fill_start.astype(jnp.int32)


def kernel(x_prompt, x_sample, state_ssm_re, state_ssm_im, cache_swa_k, cache_swa_v, cache_mem_k, cache_mem_v, mem_prompt, w_in, ssm_lam_re, ssm_lam_im, ssm_log_dt, ssm_b_re, ssm_b_im, ssm_c_re, ssm_c_im, ssm_d, w_glu, swa_sinks, rel_bias, w_o_attn, w_out, w_xq, w_mk, w_mv, w_xo, w_router, b_router, w_up, b_up, w_down, b_down, norm_mix, norm_x, norm_mem, norm_moe, norm_final):
    depth = w_in.shape[0]
    assert depth == 1, "single-layer step"
    bp, lp, d = x_prompt.shape
    bs, ls, _ = x_sample.shape
    n_p, n_s = bp * lp, bs * ls
    t = n_p + n_s
    _, _, window, kvh, hd = cache_swa_k.shape
    kv_width = kvh * hd
    n_heads = swa_sinks.shape[1]
    q_width = n_heads * hd
    ssm_width = ssm_d.shape[1]
    _, _, mem_len, x_heads, x_hd = cache_mem_k.shape
    xw = x_heads * x_hd
    ssm_groups, ssm_state = ssm_lam_re.shape[1], ssm_lam_re.shape[2]
    q_col = ssm_width
    k_col = q_col + q_width
    v_col = k_col + kv_width
    gs_col = v_col + kv_width
    ga_col = gs_col + d

    l = 0
    w_in_b = w_in[l].astype(BF16)
    w_glu_b, w_o_b, w_out_b = w_glu[l].astype(BF16), w_o_attn[l].astype(BF16), w_out[l].astype(BF16)
    w_xq_b, w_xo_b = w_xq[l].astype(BF16), w_xo[l].astype(BF16)
    w_mkv_b = jnp.concatenate([w_mk[l], w_mv[l]], axis=1).astype(BF16)
    a_re, a_im, bcat, ccat = _s5_discretize(ssm_lam_re[l], ssm_lam_im[l], ssm_log_dt[l], ssm_b_re[l],
                                            ssm_b_im[l], ssm_c_re[l], ssm_c_im[l])
    bucket = _t5_bucket_table(CHUNK, window + CHUNK, window)
    bias = jnp.transpose(rel_bias.astype(F32)[bucket], (2, 0, 1))

    def mixers(x, n_seq, seq_len, s0_re, s0_im, k_cache, v_cache):
        xf = x.reshape(n_seq * seq_len, d)
        h = _norm_matmul(xf, norm_mix[l], w_in_b, tm=1024, tn=512)
        y_ssm, s_re, s_im = _s5(h[:, :ssm_width], s0_re, s0_im, a_re, a_im, bcat, ccat, ssm_d[l],
                                n_seq=n_seq, seq_len=seq_len)
        attn = _swa(h, k_cache, v_cache, swa_sinks[l], bias, n_seq=n_seq, seq_len=seq_len, q_col=q_col,
                    k_col=k_col, v_col=v_col, q_width=q_width, kv_width=kv_width, window=window)
        x1 = _merge(xf, y_ssm, attn, h, w_glu_b, w_o_b, w_out_b, gs_col=gs_col, ga_col=ga_col, tm=256)
        h3 = h.reshape(n_seq, seq_len, -1)
        k_new = h3[:, :, k_col:k_col + kv_width]
        v_new = h3[:, :, v_col:v_col + kv_width]
        if k_cache is not None:
            k_new = jnp.concatenate([k_cache.reshape(n_seq, window, kv_width), k_new], axis=1)
            v_new = jnp.concatenate([v_cache.reshape(n_seq, window, kv_width), v_new], axis=1)
        k_new = k_new[:, -window:].reshape(1, n_seq, window, kvh, hd)
        v_new = v_new[:, -window:].reshape(1, n_seq, window, kvh, hd)
        state_shape = (1, n_seq, ssm_groups, ssm_state)
        return x1, s_re.reshape(state_shape), s_im.reshape(state_shape), k_new, v_new

    zero_state = jnp.zeros((bp, ssm_groups * ssm_state), F32)
    x1_p, sre_p, sim_p, k_p, v_p = mixers(x_prompt, bp, lp, zero_state, zero_state, None, None)
    x1_s, sre_s, sim_s, k_s, v_s = mixers(x_sample, bs, ls, state_ssm_re[l], state_ssm_im[l],
                                          cache_swa_k[l], cache_swa_v[l])

    mkv = _norm_matmul(mem_prompt.reshape(bp * mem_len, d), norm_mem[l], w_mkv_b, tm=1024, tn=512)
    mkv = mkv.reshape(bp, mem_len, 2 * xw)
    mk_p, mv_p = mkv[:, :, :xw], mkv[:, :, xw:]

    x2_p = _xattn(x1_p, norm_x[l], mk_p, mv_p, w_xq_b, w_xo_b, n_seq=bp, seq_len=lp, n_heads=x_heads)
    x2_s = _xattn(x1_s, norm_x[l], cache_mem_k[l].reshape(bs, mem_len, xw), cache_mem_v[l].reshape(bs, mem_len, xw),
                  w_xq_b, w_xo_b, n_seq=bs, seq_len=ls, n_heads=x_heads)

    n_exp = w_router.shape[2]
    xn, idx, gate, rank, counts = _router(x2_p, x2_s, norm_moe[l], w_router[l], b_router[l])
    n_tiles = (t * TOP_K) // MOE_TM + n_exp
    pos, tile_expert, tile_valid, last_tile, fill_start = _moe_layout(idx, rank, counts[:, 0], n_tiles)
    xs = _dispatch(xn, pos, fill_start, n_tiles * MOE_TM)
    ys = _experts(xs, tile_expert, tile_valid, last_tile, w_up, b_up, w_down, b_down)
    y_p, y_s = _combine(x2_p, x2_s, ys, pos, gate.T, norm_final)

    return (y_p.reshape(bp, lp, d), y_s.reshape(bs, ls, d), sre_p, sim_p, k_p, v_p,
            mk_p.reshape(1, bp, mem_len, x_heads, x_hd), mv_p.reshape(1, bp, mem_len, x_heads, x_hd),
            sre_s, sim_s, k_s, v_s)
```

```python
import functools
import math

import numpy as np
import jax
import jax.numpy as jnp
from jax import lax
from jax.experimental import pallas as pl
from jax.experimental.pallas import tpu as pltpu

F32 = jnp.float32
BF16 = jnp.bfloat16

LANE = 128
SUBLANE = 8
VMEM_LIMIT = 56 * 1024 * 1024

CHUNK = 64
SSM_GROUP = 16
SWA_GROUP = 4
REL_BUCKETS = 32
REL_MAX_DIST = 128
TOP_K = 4
SWIGLU_LIMIT = 7.0
SWIGLU_ALPHA = 1.702
RMS_EPS = 1e-5
SEQ_PER_SCAN = SUBLANE

S5_STRIP = 512
MOE_CAP = 2560
MOE_SB = 256
MOE_TF = 256
TOK_TM = 256
ROUTER_TM = 512


def _params(sem, vmem=VMEM_LIMIT):
    return pltpu.CompilerParams(dimension_semantics=sem, vmem_limit_bytes=vmem)


def _rms(x, g):
    return x * lax.rsqrt(jnp.mean(x * x, axis=-1, keepdims=True) + RMS_EPS) * g


def _resident(shape):
    nd = len(shape)
    return pl.BlockSpec(shape, lambda *_: (0,) * nd, pipeline_mode=pl.Buffered(1))


def _norm_matmul_body(x_ref, g_ref, w_ref, o_ref, xn_ref):
    @pl.when(pl.program_id(1) == 0)
    def _():
        xn_ref[...] = _rms(x_ref[...], g_ref[...]).astype(BF16)

    o_ref[...] = jnp.dot(xn_ref[...], w_ref[...], preferred_element_type=F32)


def _norm_matmul(x, g, w, *, tm, tn):
    m, d = x.shape
    n = w.shape[1]
    tm = min(tm, m)
    assert m % tm == 0 and n % tn == 0
    return pl.pallas_call(
        _norm_matmul_body,
        grid=(m // tm, n // tn),
        in_specs=[
            pl.BlockSpec((tm, d), lambda i, j: (i, 0)),
            pl.BlockSpec((1, d), lambda i, j: (0, 0)),
            pl.BlockSpec((d, tn), lambda i, j: (0, j)),
        ],
        out_specs=pl.BlockSpec((tm, tn), lambda i, j: (i, j)),
        out_shape=jax.ShapeDtypeStruct((m, n), F32),
        scratch_shapes=[pltpu.VMEM((tm, d), BF16)],
        compiler_params=_params(("parallel", "arbitrary")),
        name="norm_matmul",
    )(x, g.reshape(1, d), w)


def _s5_discretize(lam_re, lam_im, log_dt, b_re, b_im, c_re, c_im):
    g, p = lam_re.shape
    gpb = LANE // SSM_GROUP
    nblk = g // gpb
    lr, li = lam_re.astype(F32), lam_im.astype(F32)
    dt = jnp.exp(log_dt.astype(F32))[:, None]
    mag = jnp.exp(lr * dt)
    a_re = mag * jnp.cos(li * dt)
    a_im = mag * jnp.sin(li * dt)
    den = lr * lr + li * li
    f_re = ((a_re - 1.0) * lr + a_im * li) / den
    f_im = (a_im * lr - (a_re - 1.0) * li) / den
    br, bi = b_re.astype(F32), b_im.astype(F32)
    bb_re = f_re[..., None] * br - f_im[..., None] * bi
    bb_im = f_re[..., None] * bi + f_im[..., None] * br
    eye = jnp.eye(gpb, dtype=F32)

    def b_blocks(bb):
        t = jnp.transpose(bb, (0, 2, 1)).reshape(nblk, gpb, SSM_GROUP, p)
        return jnp.einsum("jghp,gk->jghkp", t, eye).reshape(nblk, gpb * SSM_GROUP, gpb * p)

    def c_blocks(c):
        t = c.astype(F32).reshape(nblk, gpb, SSM_GROUP, p)
        return jnp.einsum("jghp,gk->jgpkh", t, eye).reshape(nblk, gpb * p, gpb * SSM_GROUP)

    bcat = jnp.concatenate([b_blocks(bb_re), b_blocks(bb_im)], axis=2).astype(BF16)
    ccat = jnp.concatenate([c_blocks(c_re), -c_blocks(c_im)], axis=1).astype(BF16)
    bro = lambda a: jnp.broadcast_to(a.reshape(1, g * p), (SUBLANE, g * p))
    return bro(a_re), bro(a_im), bcat, ccat


def _s5_body(u_ref, s0re_ref, s0im_ref, are_ref, aim_ref, bcat_ref, ccat_ref, d_ref,
             y_ref, ore_ref, oim_ref, sre, sim, st_re, st_im, *, tc, nblk, blk_state):
    c = pl.program_id(1)

    @pl.when(c == 0)
    def _():
        st_re[...] = s0re_ref[...]
        st_im[...] = s0im_ref[...]

    for j in range(nblk):
        ub = u_ref[:, j * LANE:(j + 1) * LANE].astype(BF16)
        bu = jnp.dot(ub, bcat_ref[j], preferred_element_type=F32)
        sre[:, j * blk_state:(j + 1) * blk_state] = bu[:, :blk_state]
        sim[:, j * blk_state:(j + 1) * blk_state] = bu[:, blk_state:]

    n_state = sre.shape[1]
    for strip in range(n_state // S5_STRIP):
        sl = slice(strip * S5_STRIP, (strip + 1) * S5_STRIP)
        ar = are_ref[:, sl]
        ai = aim_ref[:, sl]

        def step(t, carry, sl=sl, ar=ar, ai=ai):
            sr, si = carry
            rows = pl.ds(pl.multiple_of(t * SUBLANE, SUBLANE), SUBLANE)
            nr = ar * sr - ai * si + sre[rows, sl]
            ni = ar * si + ai * sr + sim[rows, sl]
            sre[rows, sl] = nr
            sim[rows, sl] = ni
            return nr, ni

        sr, si = lax.fori_loop(0, tc, step, (st_re[:, sl], st_im[:, sl]), unroll=4)
        st_re[:, sl] = sr
        st_im[:, sl] = si

    for j in range(nblk):
        cols = slice(j * blk_state, (j + 1) * blk_state)
        scat = jnp.concatenate([sre[:, cols].astype(BF16), sim[:, cols].astype(BF16)], axis=1)
        yj = jnp.dot(scat, ccat_ref[j], preferred_element_type=F32)
        lanes = slice(j * LANE, (j + 1) * LANE)
        yj = yj + d_ref[:, lanes] * u_ref[:, lanes]
        y_ref[:, lanes] = jax.nn.gelu(yj).astype(y_ref.dtype)

    @pl.when(c == pl.num_programs(1) - 1)
    def _():
        ore_ref[...] = st_re[...]
        oim_ref[...] = st_im[...]


def _s5(u, s0_re, s0_im, a_re, a_im, bcat, ccat, d_skip, *, n_seq, seq_len):
    w = u.shape[1]
    n_state = a_re.shape[1]
    nblk = bcat.shape[0]
    ng = n_seq // SEQ_PER_SCAN
    tc = min(seq_len, 32)
    rows = tc * SEQ_PER_SCAN
    u_tb = u.reshape(ng, SEQ_PER_SCAN, seq_len, w).transpose(0, 2, 1, 3).reshape(ng, seq_len * SEQ_PER_SCAN, w)
    s0_re = s0_re.astype(F32).reshape(ng, SEQ_PER_SCAN, n_state)
    s0_im = s0_im.astype(F32).reshape(ng, SEQ_PER_SCAN, n_state)
    state_spec = pl.BlockSpec((None, SEQ_PER_SCAN, n_state), lambda g, c: (g, 0, 0))
    y_tb, o_re, o_im = pl.pallas_call(
        functools.partial(_s5_body, tc=tc, nblk=nblk, blk_state=n_state // nblk),
        grid=(ng, seq_len // tc),
        in_specs=[
            pl.BlockSpec((None, rows, w), lambda g, c: (g, c, 0)),
            state_spec,
            state_spec,
            _resident(a_re.shape),
            _resident(a_im.shape),
            _resident(bcat.shape),
            _resident(ccat.shape),
            _resident((1, w)),
        ],
        out_specs=[
            pl.BlockSpec((None, rows, w), lambda g, c: (g, c, 0)),
            state_spec,
            state_spec,
        ],
        out_shape=[
            jax.ShapeDtypeStruct((ng, seq_len * SEQ_PER_SCAN, w), BF16),
            jax.ShapeDtypeStruct((ng, SEQ_PER_SCAN, n_state), F32),
            jax.ShapeDtypeStruct((ng, SEQ_PER_SCAN, n_state), F32),
        ],
        scratch_shapes=[
            pltpu.VMEM((rows, n_state), F32),
            pltpu.VMEM((rows, n_state), F32),
            pltpu.VMEM((SEQ_PER_SCAN, n_state), F32),
            pltpu.VMEM((SEQ_PER_SCAN, n_state), F32),
        ],
        compiler_params=_params(("parallel", "arbitrary")),
        name="s5_scan",
    )(u_tb, s0_re, s0_im, a_re, a_im, bcat, ccat, d_skip.astype(F32).reshape(1, w))
    y = y_tb.reshape(ng, seq_len, SEQ_PER_SCAN, w).transpose(0, 2, 1, 3).reshape(n_seq * seq_len, w)
    return y, o_re.reshape(n_seq, n_state), o_im.reshape(n_seq, n_state)


def _t5_bucket_table(n_q, n_k, n_prev):
    half = REL_BUCKETS // 2
    max_exact = half // 2
    assert (REL_MAX_DIST // max_exact) ** 2 == 2 ** (half - max_exact)
    rel = np.arange(n_k)[None, :] - n_prev - np.arange(n_q)[:, None]
    n = np.abs(rel)
    k = np.zeros_like(n)
    for kk in range(1, 2 * half):
        k = np.where(n * n >= max_exact * max_exact * (2 ** kk), kk, k)
    large = np.minimum(max_exact + k, half - 1)
    return ((rel > 0) * half + np.where(n < max_exact, n, large)).astype(np.int32)


def _swa_body(*refs, rows, window, n_kv, head_dim, has_past, has_prev_block):
    it = iter(refs)
    q_ref, kc_ref, vc_ref = next(it), next(it), next(it)
    kp_ref = vp_ref = kcache_ref = vcache_ref = None
    if has_prev_block:
        kp_ref, vp_ref = next(it), next(it)
    if has_past:
        kcache_ref, vcache_ref = next(it), next(it)
    bias_ref, sink_ref = next(it), next(it)
    o_ref = next(it)
    kx, vx = next(it), next(it)

    r = pl.program_id(1)
    n_k = window + CHUNK
    grp = SWA_GROUP * head_dim

    def put(dst, lo, src):
        for kv in range(n_kv):
            dst[kv, lo:lo + src.shape[0], :] = src[:, kv * head_dim:(kv + 1) * head_dim].astype(BF16)

    @pl.when(r == 0)
    def _():
        if has_past:
            put(kx, 0, kcache_ref[...])
            put(vx, 0, vcache_ref[...])
        else:
            kx[:, 0:window, :] = jnp.zeros((n_kv, window, head_dim), BF16)
            vx[:, 0:window, :] = jnp.zeros((n_kv, window, head_dim), BF16)

    if has_prev_block:
        @pl.when(r > 0)
        def _():
            put(kx, 0, kp_ref[...])
            put(vx, 0, vp_ref[...])

    put(kx, window, kc_ref[...])
    put(vx, window, vc_ref[...])

    scale = 1.0 / math.sqrt(head_dim)
    n_chunks = rows // CHUNK
    key_pos = lax.broadcasted_iota(jnp.int32, (SWA_GROUP * CHUNK, n_k), 1)

    def chunk(cc, _):
        q0 = pl.multiple_of(cc * CHUNK, CHUNK)
        qrows = pl.ds(q0, CHUNK)
        krows = pl.ds(q0, n_k)
        if not has_past:
            valid = key_pos + (r * n_chunks + cc) * CHUNK >= window
        logits = []
        for kv in range(n_kv):
            qg = q_ref[qrows, kv * grp:(kv + 1) * grp] * scale
            qs = jnp.concatenate([qg[:, g * head_dim:(g + 1) * head_dim] for g in range(SWA_GROUP)], axis=0)
            lg = lax.dot_general(qs.astype(BF16), kx[kv, krows, :], (((1,), (1,)), ((), ())),
                                 preferred_element_type=F32) + bias_ref[kv]
            if not has_past:
                lg = jnp.where(valid, lg, -1e30)
            logits.append(lg)
        probs = []
        for kv in range(n_kv):
            sink = sink_ref[kv]
            m = jnp.maximum(jnp.max(logits[kv], axis=-1, keepdims=True), sink)
            p = jnp.exp(logits[kv] - m)
            den = jnp.sum(p, axis=-1, keepdims=True) + jnp.exp(sink - m)
            probs.append((p * (1.0 / den)).astype(BF16))
        for kv in range(n_kv):
            out = jnp.dot(probs[kv], vx[kv, krows, :], preferred_element_type=F32)
            og = jnp.concatenate([out[g * CHUNK:(g + 1) * CHUNK, :] for g in range(SWA_GROUP)], axis=1)
            o_ref[qrows, kv * grp:(kv + 1) * grp] = og.astype(o_ref.dtype)
        return 0

    lax.fori_loop(0, n_chunks, chunk, 0)


def _swa(h, k_cache, v_cache, sinks, bias, *, n_seq, seq_len, q_col, k_col, v_col, q_width, kv_width, window):
    has_past = k_cache is not None
    rows = min(seq_len, 256)
    has_prev_block = seq_len > rows
    assert rows % CHUNK == 0 and (not has_prev_block or rows % window == 0)
    n_heads = sinks.shape[0]
    head_dim = q_width // n_heads
    n_kv = n_heads // SWA_GROUP
    n_k = window + CHUNK
    blocks_per_seq = seq_len // rows
    qb, kb, vb = q_col // q_width, k_col // kv_width, v_col // kv_width
    assert q_col % q_width == 0 and k_col % kv_width == 0 and v_col % kv_width == 0
    bias_g = bias.reshape(n_kv, SWA_GROUP * CHUNK, n_k)
    sink_g = jnp.repeat(sinks.astype(F32).reshape(n_kv, SWA_GROUP), CHUNK, axis=1)[..., None]

    in_specs = [
        pl.BlockSpec((rows, q_width), lambda b, r: (b * blocks_per_seq + r, qb)),
        pl.BlockSpec((rows, kv_width), lambda b, r: (b * blocks_per_seq + r, kb)),
        pl.BlockSpec((rows, kv_width), lambda b, r: (b * blocks_per_seq + r, vb)),
    ]
    args = [h, h, h]
    if has_prev_block:
        wpb = rows // window
        prev = lambda col: pl.BlockSpec(
            (window, kv_width), lambda b, r: (jnp.maximum((b * blocks_per_seq + r) * wpb - 1, 0), col))
        in_specs += [prev(kb), prev(vb)]
        args += [h, h]
    if has_past:
        cache = pl.BlockSpec((None, window, kv_width), lambda b, r: (b, 0, 0))
        in_specs += [cache, cache]
        args += [k_cache.reshape(n_seq, window, kv_width), v_cache.reshape(n_seq, window, kv_width)]
    in_specs += [_resident(bias_g.shape), _resident(sink_g.shape)]
    args += [bias_g, sink_g]

    return pl.pallas_call(
        functools.partial(_swa_body, rows=rows, window=window, n_kv=n_kv, head_dim=head_dim,
                          has_past=has_past, has_prev_block=has_prev_block),
        grid=(n_seq, blocks_per_seq),
        in_specs=in_specs,
        out_specs=pl.BlockSpec((rows, q_width), lambda b, r: (b * blocks_per_seq + r, 0)),
        out_shape=jax.ShapeDtypeStruct((n_seq * seq_len, q_width), BF16),
        scratch_shapes=[pltpu.VMEM((n_kv, window + rows, head_dim), BF16)] * 2,
        compiler_params=_params(("parallel", "arbitrary")),
        name="swa_attention",
    )(*args)


def _merge(x, y_ssm, attn, h, w_glu, w_o, w_out, *, gs_col, ga_col, tm):
    m, d = x.shape
    gcol = math.gcd(gs_col, ga_col, d)
    assert gcol % LANE == 0
    nsub = d // gcol

    def body(x_ref, y_ref, a_ref, *rest):
        gs = jnp.concatenate([r[...] for r in rest[:nsub]], axis=1)
        ga = jnp.concatenate([r[...] for r in rest[nsub:2 * nsub]], axis=1)
        wglu_ref, wo_ref, wout_ref, o_ref = rest[2 * nsub:]
        gl = jnp.dot(y_ref[...], wglu_ref[...], preferred_element_type=F32)
        ys = gl[:, :d] * jax.nn.sigmoid(gl[:, d:])
        ya = jnp.dot(a_ref[...], wo_ref[...], preferred_element_type=F32)
        merged = jax.nn.sigmoid(gs) * ys + jax.nn.sigmoid(ga) * ya
        o_ref[...] = x_ref[...] + jnp.dot(merged.astype(BF16), wout_ref[...], preferred_element_type=F32)

    gate_specs = [pl.BlockSpec((tm, gcol), lambda i, c=c: (i, c))
                  for base in (gs_col, ga_col) for c in range(base // gcol, base // gcol + nsub)]
    return pl.pallas_call(
        body,
        grid=(m // tm,),
        in_specs=[
            pl.BlockSpec((tm, d), lambda i: (i, 0)),
            pl.BlockSpec((tm, y_ssm.shape[1]), lambda i: (i, 0)),
            pl.BlockSpec((tm, attn.shape[1]), lambda i: (i, 0)),
            *gate_specs,
            _resident(w_glu.shape),
            _resident(w_o.shape),
            _resident(w_out.shape),
        ],
        out_specs=pl.BlockSpec((tm, d), lambda i: (i, 0)),
        out_shape=jax.ShapeDtypeStruct((m, d), F32),
        compiler_params=_params(("parallel",)),
        name="merge_out_proj",
    )(x, y_ssm, attn, *([h] * (2 * nsub)), w_glu, w_o, w_out)


def _xattn_body(x_ref, g_ref, mk_ref, mv_ref, wq_ref, wo_ref, o_ref, *, n_heads, head_dim):
    x = x_ref[...]
    q = jnp.dot(_rms(x, g_ref[...]).astype(BF16), wq_ref[...], preferred_element_type=F32).astype(BF16)
    scale = 1.0 / math.sqrt(head_dim)
    outs = []
    for hd in range(n_heads):
        hs = slice(hd * head_dim, (hd + 1) * head_dim)
        logits = lax.dot_general(q[:, hs], mk_ref[:, hs].astype(BF16), (((1,), (1,)), ((), ())),
                                 preferred_element_type=F32) * scale
        m = jnp.max(logits, axis=-1, keepdims=True)
        p = jnp.exp(logits - m)
        probs = p / jnp.sum(p, axis=-1, keepdims=True)
        outs.append(jnp.dot(probs.astype(BF16), mv_ref[:, hs].astype(BF16), preferred_element_type=F32))
    att = jnp.concatenate(outs, axis=1).astype(BF16)
    o_ref[...] = x + jnp.dot(att, wo_ref[...], preferred_element_type=F32)


def _xattn(x, g, mk, mv, w_q, w_o, *, n_seq, seq_len, n_heads):
    m, d = x.shape
    tm = min(seq_len, 256)
    blocks_per_seq = seq_len // tm
    mem_len, xw = mk.shape[1], mk.shape[2]
    return pl.pallas_call(
        functools.partial(_xattn_body, n_heads=n_heads, head_dim=xw // n_heads),
        grid=(n_seq, blocks_per_seq),
        in_specs=[
            pl.BlockSpec((tm, d), lambda b, r: (b * blocks_per_seq + r, 0)),
            _resident((1, d)),
            pl.BlockSpec((None, mem_len, xw), lambda b, r: (b, 0, 0)),
            pl.BlockSpec((None, mem_len, xw), lambda b, r: (b, 0, 0)),
            _resident(w_q.shape),
            _resident(w_o.shape),
        ],
        out_specs=pl.BlockSpec((tm, d), lambda b, r: (b * blocks_per_seq + r, 0)),
        out_shape=jax.ShapeDtypeStruct((m, d), F32),
        compiler_params=_params(("parallel", "arbitrary")),
        name="cross_attention",
    )(x, g.reshape(1, d), mk, mv, w_q, w_o)


def _router_body(xp_ref, xs_ref, g_ref, wr_ref, br_ref, xn_ref, idx_ref, gate_ref, rank_ref, cnt_ref, carry,
                 *, n_exp, n_prompt_tiles):
    i = pl.program_id(0)

    @pl.when(i == 0)
    def _():
        carry[...] = jnp.zeros_like(carry)

    xn = _rms(jnp.where(i < n_prompt_tiles, xp_ref[...], xs_ref[...]), g_ref[...])
    xn_ref[...] = xn
    tm = xn.shape[0]
    logits = lax.dot_general(wr_ref[...], xn, (((1,), (1,)), ((), ())),
                             precision=lax.Precision.HIGHEST, preferred_element_type=F32) + br_ref[...]
    eid = lax.broadcasted_iota(jnp.int32, (n_exp, tm), 0)
    vals, hots = [], []
    work = logits
    for k in range(TOP_K):
        m = jnp.max(work, axis=0, keepdims=True)
        sel = jnp.min(jnp.where(work == m, eid, n_exp), axis=0, keepdims=True)
        hot = eid == sel
        vals.append(m)
        hots.append(hot)
        idx_ref[k:k + 1, :] = sel
        work = jnp.where(hot, -jnp.inf, work)
    ex = [jnp.exp(v - vals[0]) for v in vals]
    tot = ex[0] + ex[1] + ex[2] + ex[3]
    for k in range(TOP_K):
        gate_ref[k:k + 1, :] = ex[k] / tot

    chosen = (hots[0] | hots[1] | hots[2] | hots[3])
    a = jnp.where(chosen, 1.0, 0.0)
    earlier = (lax.broadcasted_iota(jnp.int32, (tm, tm), 0) < lax.broadcasted_iota(jnp.int32, (tm, tm), 1))
    prefix = jnp.dot(a.astype(BF16), jnp.where(earlier, 1.0, 0.0).astype(BF16),
                     preferred_element_type=F32) + carry[...]
    for k in range(TOP_K):
        rank_ref[k:k + 1, :] = jnp.sum(jnp.where(hots[k], prefix, 0.0), axis=0, keepdims=True).astype(jnp.int32)
    carry[...] = carry[...] + jnp.sum(a, axis=1, keepdims=True)
    cnt_ref[...] = carry[...].astype(jnp.int32)


def _two_group_specs(tm, d, npt):
    return [pl.BlockSpec((tm, d), lambda i: (jnp.minimum(i, npt - 1), 0)),
            pl.BlockSpec((tm, d), lambda i: (jnp.maximum(i - npt, 0), 0))]


def _router(x_p, x_s, g, w_router, b_router):
    d = x_p.shape[1]
    t = x_p.shape[0] + x_s.shape[0]
    n_exp = w_router.shape[1]
    tm = ROUTER_TM
    assert x_p.shape[0] % tm == 0 and x_s.shape[0] % tm == 0
    npt = x_p.shape[0] // tm
    row = pl.BlockSpec((TOP_K, tm), lambda i: (0, i))
    return pl.pallas_call(
        functools.partial(_router_body, n_exp=n_exp, n_prompt_tiles=npt),
        grid=(t // tm,),
        in_specs=[
            *_two_group_specs(tm, d, npt),
            _resident((1, d)),
            _resident((n_exp, d)),
            _resident((n_exp, 1)),
        ],
        out_specs=[pl.BlockSpec((tm, d), lambda i: (i, 0)), row, row, row,
                   pl.BlockSpec((n_exp, 1), lambda i: (0, 0))],
        out_shape=[
            jax.ShapeDtypeStruct((t, d), F32),
            jax.ShapeDtypeStruct((TOP_K, t), jnp.int32),
            jax.ShapeDtypeStruct((TOP_K, t), F32),
            jax.ShapeDtypeStruct((TOP_K, t), jnp.int32),
            jax.ShapeDtypeStruct((n_exp, 1), jnp.int32),
        ],
        scratch_shapes=[pltpu.VMEM((n_exp, 1), F32)],
        compiler_params=_params(("arbitrary",)),
        name="moe_router",
    )(x_p, x_s, g.reshape(1, d), w_router.astype(F32).T, b_router.astype(F32).reshape(n_exp, 1))


def _dispatch_body(pos_ref, fill_ref, x_ref, xs_ref, zeros, sem, fill_sem, *, n_exp):
    tm, d = x_ref.shape
    fill_rows = zeros.shape[0]

    def fill_copy(e):
        start = pl.multiple_of(fill_ref[e], SUBLANE)
        return pltpu.make_async_copy(zeros, xs_ref.at[pl.ds(start, fill_rows), :], fill_sem)

    @pl.when(pl.program_id(0) == 0)
    def _():
        zeros[...] = jnp.zeros_like(zeros)
        for e in range(n_exp):
            fill_copy(e).start()
        for e in range(n_exp):
            fill_copy(e).wait()

    def issue(t, _):
        for k in range(TOP_K):
            pltpu.make_async_copy(x_ref.at[pl.ds(t, 1), :], xs_ref.at[pl.ds(pos_ref[k, t], 1), :], sem).start()
        return 0

    lax.fori_loop(0, tm, issue, 0, unroll=8)
    for k in range(TOP_K):
        pltpu.make_async_copy(x_ref, xs_ref.at[pl.ds(0, tm), :], sem).wait()


def _dispatch(xn, pos, fill_start, n_rows):
    t, d = xn.shape
    tm = TOK_TM
    n_exp = fill_start.shape[0]
    fill_rows = MOE_SB + SUBLANE
    return pl.pallas_call(
        functools.partial(_dispatch_body, n_exp=n_exp),
        grid=(t // tm,),
        in_specs=[
            pl.BlockSpec((TOP_K, tm), lambda i: (0, i), memory_space=pltpu.SMEM),
            pl.BlockSpec(memory_space=pltpu.SMEM),
            pl.BlockSpec((tm, d), lambda i: (i, 0)),
        ],
        out_specs=pl.BlockSpec(memory_space=pl.ANY),
        out_shape=jax.ShapeDtypeStruct((n_rows + fill_rows, d), xn.dtype),
        scratch_shapes=[pltpu.VMEM((fill_rows, d), xn.dtype), pltpu.SemaphoreType.DMA, pltpu.SemaphoreType.DMA],
        compiler_params=_params(("arbitrary",)),
        name="moe_dispatch",
    )(pos, fill_start, xn)


def _experts_body(te_ref, row0_ref, nv_ref, xs_ref, wg_ref, wl_ref, bg_ref, bl_ref, wd_ref, bd_ref,
                  ys_ref, acc, xb, stage, wgb, wlb, wdb, in_sem, out_sem, *, n_j):
    del te_ref
    i, j = pl.program_id(0), pl.program_id(1)
    n_valid = nv_ref[i]
    row0 = row0_ref[i]
    d = acc.shape[1]
    n_sub = (n_valid + MOE_SB - 1) // MOE_SB

    def hbm_rows(s):
        return pl.ds(pl.multiple_of(row0 + s * MOE_SB, MOE_SB), MOE_SB)

    def vmem_rows(s):
        return pl.ds(pl.multiple_of(s * MOE_SB, MOE_SB), MOE_SB)

    def in_copy(s, slot):
        return pltpu.make_async_copy(xs_ref.at[hbm_rows(s), :], stage.at[slot], in_sem.at[slot])

    def out_copy(s):
        return pltpu.make_async_copy(acc.at[vmem_rows(s), :], ys_ref.at[hbm_rows(s), :], out_sem)

    @pl.when(n_valid > 0)
    def _():
        @pl.when(j == 0)
        def _():
            in_copy(0, 0).start()

            def load(s, _):
                slot = s % 2

                @pl.when(s + 1 < n_sub)
                def _():
                    in_copy(s + 1, 1 - slot).start()

                in_copy(s, slot).wait()
                xb[vmem_rows(s), :] = stage[slot].astype(BF16)
                acc[vmem_rows(s), :] = jnp.broadcast_to(bd_ref[...], (MOE_SB, d))
                return 0

            lax.fori_loop(0, n_sub, load, 0)

        wgb[...] = wg_ref[...].astype(BF16)
        wlb[...] = wl_ref[...].astype(BF16)
        wdb[...] = wd_ref[...].astype(BF16)

        def sub_block(s, _):
            rows = vmem_rows(s)
            xs = xb[rows, :]
            hg = jnp.dot(xs, wgb[...], preferred_element_type=F32) + bg_ref[...]
            hl = jnp.dot(xs, wlb[...], preferred_element_type=F32) + bl_ref[...]
            glu = jnp.minimum(hg, SWIGLU_LIMIT)
            lin = jnp.clip(hl, -SWIGLU_LIMIT, SWIGLU_LIMIT)
            act = glu * jax.nn.sigmoid(SWIGLU_ALPHA * glu) * (lin + 1.0)
            acc[rows, :] += jnp.dot(act.astype(BF16), wdb[...], preferred_element_type=F32)

            @pl.when(j == n_j - 1)
            def _():
                out_copy(s).start()

            return 0

        lax.fori_loop(0, n_sub, sub_block, 0)

        @pl.when(j == n_j - 1)
        def _():
            def drain(s, _):
                out_copy(s).wait()
                return 0

            lax.fori_loop(0, n_sub, drain, 0)


def _experts(xs, tile_expert, tile_row0, tile_valid, n_rows, w_up, b_up, w_down, b_down):
    d = xs.shape[1]
    n_tiles = tile_expert.shape[0]
    depth, n_exp, _, ff2 = w_up.shape
    assert depth == 1
    d_ff = ff2 // 2
    n_j = d_ff // MOE_TF
    b_up = b_up.reshape(depth, n_exp, 1, ff2)
    b_down = b_down.reshape(depth, n_exp, 1, d)

    def jsel(i, j, nv):
        return jnp.where(nv[i] > 0, j, n_j - 1)

    grid_spec = pltpu.PrefetchScalarGridSpec(
        num_scalar_prefetch=3,
        grid=(n_tiles, n_j),
        in_specs=[
            pl.BlockSpec(memory_space=pl.ANY),
            pl.BlockSpec((None, None, d, MOE_TF), lambda i, j, te, r0, nv: (0, te[i], 0, jsel(i, j, nv))),
            pl.BlockSpec((None, None, d, MOE_TF), lambda i, j, te, r0, nv: (0, te[i], 0, n_j + jsel(i, j, nv))),
            pl.BlockSpec((None, None, 1, MOE_TF), lambda i, j, te, r0, nv: (0, te[i], 0, jsel(i, j, nv))),
            pl.BlockSpec((None, None, 1, MOE_TF), lambda i, j, te, r0, nv: (0, te[i], 0, n_j + jsel(i, j, nv))),
            pl.BlockSpec((None, None, MOE_TF, d), lambda i, j, te, r0, nv: (0, te[i], jsel(i, j, nv), 0)),
            pl.BlockSpec((None, None, 1, d), lambda i, j, te, r0, nv: (0, te[i], 0, 0)),
        ],
        out_specs=pl.BlockSpec(memory_space=pl.ANY),
        scratch_shapes=[
            pltpu.VMEM((MOE_CAP, d), F32),
            pltpu.VMEM((MOE_CAP, d), BF16),
            pltpu.VMEM((2, MOE_SB, d), F32),
            pltpu.VMEM((d, MOE_TF), BF16),
            pltpu.VMEM((d, MOE_TF), BF16),
            pltpu.VMEM((MOE_TF, d), BF16),
            pltpu.SemaphoreType.DMA((2,)),
            pltpu.SemaphoreType.DMA,
        ],
    )
    return pl.pallas_call(
        functools.partial(_experts_body, n_j=n_j),
        grid_spec=grid_spec,
        out_shape=jax.ShapeDtypeStruct((n_rows, d), F32),
        compiler_params=_params(("arbitrary", "arbitrary")),
        name="moe_experts",
    )(tile_expert, tile_row0, tile_valid, xs, w_up, w_up, b_up, b_up, w_down, b_down)


def _combine_body(pos_ref, xp_ref, xs_ref, gate_ref, g_ref, ys_ref, op_ref, os_ref, buf, sem, *, n_prompt_tiles):
    i = pl.program_id(0)
    tm = xp_ref.shape[0]

    def issue(t, _):
        for k in range(TOP_K):
            pltpu.make_async_copy(ys_ref.at[pl.ds(pos_ref[k, t], 1), :], buf.at[k, pl.ds(t, 1), :], sem).start()
        return 0

    lax.fori_loop(0, tm, issue, 0, unroll=8)
    for k in range(TOP_K):
        pltpu.make_async_copy(ys_ref.at[pl.ds(0, tm), :], buf.at[k], sem).wait()

    out = jnp.where(i < n_prompt_tiles, xp_ref[...], xs_ref[...])
    for k in range(TOP_K):
        out = out + gate_ref[:, k:k + 1] * buf[k]
    y = _rms(out, g_ref[...])

    @pl.when(i < n_prompt_tiles)
    def _():
        op_ref[...] = y

    @pl.when(i >= n_prompt_tiles)
    def _():
        os_ref[...] = y


def _combine(x_p, x_s, ys, pos, gate_t, g_final):
    n_prompt, d = x_p.shape
    t = n_prompt + x_s.shape[0]
    tm = TOK_TM
    assert n_prompt % tm == 0 and t % tm == 0
    npt = n_prompt // tm
    return pl.pallas_call(
        functools.partial(_combine_body, n_prompt_tiles=npt),
        grid=(t // tm,),
        in_specs=[
            pl.BlockSpec((TOP_K, tm), lambda i: (0, i), memory_space=pltpu.SMEM),
            *_two_group_specs(tm, d, npt),
            pl.BlockSpec((tm, TOP_K), lambda i: (i, 0)),
            _resident((1, d)),
            pl.BlockSpec(memory_space=pl.ANY),
        ],
        out_specs=_two_group_specs(tm, d, npt),
        out_shape=[
            jax.ShapeDtypeStruct((n_prompt, d), F32),
            jax.ShapeDtypeStruct((t - n_prompt, d), F32),
        ],
        scratch_shapes=[pltpu.VMEM((TOP_K, tm, d), F32), pltpu.SemaphoreType.DMA],
        compiler_params=_params(("arbitrary",)),
        name="moe_combine",
    )(pos, x_p, x_s, gate_t, g_final.reshape(1, d), ys)


def _moe_layout(idx, rank, counts, n_tiles):
    n_exp = counts.shape[0]
    region = ((counts + MOE_SB - 1) // MOE_SB) * MOE_SB
    row_start = jnp.cumsum(region) - region
    tiles_per_exp = (counts + MOE_CAP - 1) // MOE_CAP
    tile_end = jnp.cumsum(tiles_per_exp)
    tile_start = tile_end - tiles_per_exp
    n_used = tile_end[-1]
    tiles = jnp.arange(n_tiles, dtype=jnp.int32)
    te = jnp.minimum(jnp.searchsorted(tile_end, tiles, side="right"), n_exp - 1).astype(jnp.int32)
    te = jnp.where(tiles < n_used, te, te[jnp.maximum(n_used - 1, 0)])
    first = (tiles - tile_start[te]) * MOE_CAP
    valid = jnp.where(tiles < n_used, jnp.clip(counts[te] - first, 0, MOE_CAP), 0)
    row0 = jnp.where(tiles < n_used, row_start[te] + first, 0)
    hot = idx[..., None] == jnp.arange(n_exp, dtype=jnp.int32)
    pos = rank + jnp.sum(jnp.where(hot, row_start, 0), axis=-1)
    fill_start = row_start + (counts // SUBLANE) * SUBLANE
    i32 = lambda a: a.astype(jnp.int32)
    return i32(pos), te, i32(row0), i32(valid), i32(fill_start)


def kernel(x_prompt, x_sample, state_ssm_re, state_ssm_im, cache_swa_k, cache_swa_v, cache_mem_k, cache_mem_v, mem_prompt, w_in, ssm_lam_re, ssm_lam_im, ssm_log_dt, ssm_b_re, ssm_b_im, ssm_c_re, ssm_c_im, ssm_d, w_glu, swa_sinks, rel_bias, w_o_attn, w_out, w_xq, w_mk, w_mv, w_xo, w_router, b_router, w_up, b_up, w_down, b_down, norm_mix, norm_x, norm_mem, norm_moe, norm_final):
    depth = w_in.shape[0]
    assert depth == 1, "single-layer step"
    bp, lp, d = x_prompt.shape
    bs, ls, _ = x_sample.shape
    n_p, n_s = bp * lp, bs * ls
    t = n_p + n_s
    _, _, window, kvh, hd = cache_swa_k.shape
    kv_width = kvh * hd
    n_heads = swa_sinks.shape[1]
    q_width = n_heads * hd
    ssm_width = ssm_d.shape[1]
    _, _, mem_len, x_heads, x_hd = cache_mem_k.shape
    xw = x_heads * x_hd
    ssm_groups, ssm_state = ssm_lam_re.shape[1], ssm_lam_re.shape[2]
    q_col = ssm_width
    k_col = q_col + q_width
    v_col = k_col + kv_width
    gs_col = v_col + kv_width
    ga_col = gs_col + d

    l = 0
    w_in_b = w_in[l].astype(BF16)
    w_glu_b, w_o_b, w_out_b = w_glu[l].astype(BF16), w_o_attn[l].astype(BF16), w_out[l].astype(BF16)
    w_xq_b, w_xo_b = w_xq[l].astype(BF16), w_xo[l].astype(BF16)
    w_mkv_b = jnp.concatenate([w_mk[l], w_mv[l]], axis=1).astype(BF16)
    a_re, a_im, bcat, ccat = _s5_discretize(ssm_lam_re[l], ssm_lam_im[l], ssm_log_dt[l], ssm_b_re[l],
                                            ssm_b_im[l], ssm_c_re[l], ssm_c_im[l])
    bucket = _t5_bucket_table(CHUNK, window + CHUNK, window)
    bias = jnp.transpose(rel_bias.astype(F32)[bucket], (2, 0, 1))

    def mixers(x, n_seq, seq_len, s0_re, s0_im, k_cache, v_cache):
        xf = x.reshape(n_seq * seq_len, d)
        h = _norm_matmul(xf, norm_mix[l], w_in_b, tm=1024, tn=512)
        y_ssm, s_re, s_im = _s5(h[:, :ssm_width], s0_re, s0_im, a_re, a_im, bcat, ccat, ssm_d[l],
                                n_seq=n_seq, seq_len=seq_len)
        attn = _swa(h, k_cache, v_cache, swa_sinks[l], bias, n_seq=n_seq, seq_len=seq_len, q_col=q_col,
                    k_col=k_col, v_col=v_col, q_width=q_width, kv_width=kv_width, window=window)
        x1 = _merge(xf, y_ssm, attn, h, w_glu_b, w_o_b, w_out_b, gs_col=gs_col, ga_col=ga_col, tm=256)
        h3 = h.reshape(n_seq, seq_len, -1)
        k_new = h3[:, :, k_col:k_col + kv_width]
        v_new = h3[:, :, v_col:v_col + kv_width]
        if k_cache is not None:
            k_new = jnp.concatenate([k_cache.reshape(n_seq, window, kv_width), k_new], axis=1)
            v_new = jnp.concatenate([v_cache.reshape(n_seq, window, kv_width), v_new], axis=1)
        k_new = k_new[:, -window:].reshape(1, n_seq, window, kvh, hd)
        v_new = v_new[:, -window:].reshape(1, n_seq, window, kvh, hd)
        state_shape = (1, n_seq, ssm_groups, ssm_state)
        return x1, s_re.reshape(state_shape), s_im.reshape(state_shape), k_new, v_new

    zero_state = jnp.zeros((bp, ssm_groups * ssm_state), F32)
    x1_p, sre_p, sim_p, k_p, v_p = mixers(x_prompt, bp, lp, zero_state, zero_state, None, None)
    x1_s, sre_s, sim_s, k_s, v_s = mixers(x_sample, bs, ls, state_ssm_re[l], state_ssm_im[l],
                                          cache_swa_k[l], cache_swa_v[l])

    mkv = _norm_matmul(mem_prompt.reshape(bp * mem_len, d), norm_mem[l], w_mkv_b, tm=1024, tn=512)
    mkv = mkv.reshape(bp, mem_len, 2 * xw)
    mk_p, mv_p = mkv[:, :, :xw], mkv[:, :, xw:]

    x2_p = _xattn(x1_p, norm_x[l], mk_p, mv_p, w_xq_b, w_xo_b, n_seq=bp, seq_len=lp, n_heads=x_heads)
    x2_s = _xattn(x1_s, norm_x[l], cache_mem_k[l].reshape(bs, mem_len, xw), cache_mem_v[l].reshape(bs, mem_len, xw),
                  w_xq_b, w_xo_b, n_seq=bs, seq_len=ls, n_heads=x_heads)

    n_exp = w_router.shape[2]
    xn, idx, gate, rank, counts = _router(x2_p, x2_s, norm_moe[l], w_router[l], b_router[l])
    n_tiles = (t * TOP_K) // MOE_CAP + n_exp
    n_rows = t * TOP_K + n_exp * MOE_SB
    pos, tile_expert, tile_row0, tile_valid, fill_start = _moe_layout(idx, rank, counts[:, 0], n_tiles)
    xs = _dispatch(xn, pos, fill_start, n_rows)
    ys = _experts(xs, tile_expert, tile_row0, tile_valid, n_rows, w_up, b_up, w_down, b_down)
    y_p, y_s = _combine(x2_p, x2_s, ys, pos, gate.T, norm_final)

    return (y_p.reshape(bp, lp, d), y_s.reshape(bs, ls, d), sre_p, sim_p, k_p, v_p,
            mk_p.reshape(1, bp, mem_len, x_heads, x_hd), mv_p.reshape(1, bp, mem_len, x_heads, x_hd),
            sre_s, sim_s, k_s, v_s)
```

```python
import functools
import math

import numpy as np
import jax
import jax.numpy as jnp
from jax import lax
from jax.experimental import pallas as pl
from jax.experimental.pallas import tpu as pltpu

F32 = jnp.float32
BF16 = jnp.bfloat16

LANE = 128
SUBLANE = 8
VMEM_LIMIT = 56 * 1024 * 1024

CHUNK = 64
SSM_GROUP = 16
SWA_GROUP = 4
REL_BUCKETS = 32
REL_MAX_DIST = 128
TOP_K = 4
SWIGLU_LIMIT = 7.0
SWIGLU_ALPHA = 1.702
RMS_EPS = 1e-5
SEQ_PER_SCAN = SUBLANE

S5_STRIP = 512
MOE_CAP = 2560
MOE_SB = 256
MOE_BIG = 1024
MOE_DOWN_COLS = 512
MOE_TF = 256
TOK_TM = 256
ROUTER_TM = 512


def _params(sem, vmem=VMEM_LIMIT):
    return pltpu.CompilerParams(dimension_semantics=sem, vmem_limit_bytes=vmem)


def _rms(x, g):
    return x * lax.rsqrt(jnp.mean(x * x, axis=-1, keepdims=True) + RMS_EPS) * g


def _resident(shape):
    nd = len(shape)
    return pl.BlockSpec(shape, lambda *_: (0,) * nd, pipeline_mode=pl.Buffered(1))


def _norm_matmul_body(x_ref, g_ref, w_ref, o_ref, xn_ref):
    @pl.when(pl.program_id(1) == 0)
    def _():
        xn_ref[...] = _rms(x_ref[...], g_ref[...]).astype(BF16)

    o_ref[...] = jnp.dot(xn_ref[...], w_ref[...], preferred_element_type=F32)


def _norm_matmul(x, g, w, *, tm, tn):
    m, d = x.shape
    n = w.shape[1]
    tm = min(tm, m)
    assert m % tm == 0 and n % tn == 0
    return pl.pallas_call(
        _norm_matmul_body,
        grid=(m // tm, n // tn),
        in_specs=[
            pl.BlockSpec((tm, d), lambda i, j: (i, 0)),
            pl.BlockSpec((1, d), lambda i, j: (0, 0)),
            pl.BlockSpec((d, tn), lambda i, j: (0, j)),
        ],
        out_specs=pl.BlockSpec((tm, tn), lambda i, j: (i, j)),
        out_shape=jax.ShapeDtypeStruct((m, n), F32),
        scratch_shapes=[pltpu.VMEM((tm, d), BF16)],
        compiler_params=_params(("parallel", "arbitrary")),
        name="norm_matmul",
    )(x, g.reshape(1, d), w)


def _s5_discretize(lam_re, lam_im, log_dt, b_re, b_im, c_re, c_im):
    g, p = lam_re.shape
    gpb = LANE // SSM_GROUP
    nblk = g // gpb
    lr, li = lam_re.astype(F32), lam_im.astype(F32)
    dt = jnp.exp(log_dt.astype(F32))[:, None]
    mag = jnp.exp(lr * dt)
    a_re = mag * jnp.cos(li * dt)
    a_im = mag * jnp.sin(li * dt)
    den = lr * lr + li * li
    f_re = ((a_re - 1.0) * lr + a_im * li) / den
    f_im = (a_im * lr - (a_re - 1.0) * li) / den
    br, bi = b_re.astype(F32), b_im.astype(F32)
    bb_re = f_re[..., None] * br - f_im[..., None] * bi
    bb_im = f_re[..., None] * bi + f_im[..., None] * br
    eye = jnp.eye(gpb, dtype=F32)

    def b_blocks(bb):
        t = jnp.transpose(bb, (0, 2, 1)).reshape(nblk, gpb, SSM_GROUP, p)
        return jnp.einsum("jghp,gk->jghkp", t, eye).reshape(nblk, gpb * SSM_GROUP, gpb * p)

    def c_blocks(c):
        t = c.astype(F32).reshape(nblk, gpb, SSM_GROUP, p)
        return jnp.einsum("jghp,gk->jgpkh", t, eye).reshape(nblk, gpb * p, gpb * SSM_GROUP)

    bcat = jnp.concatenate([b_blocks(bb_re), b_blocks(bb_im)], axis=2).astype(BF16)
    ccat = jnp.concatenate([c_blocks(c_re), -c_blocks(c_im)], axis=1).astype(BF16)
    bro = lambda a: jnp.broadcast_to(a.reshape(1, g * p), (SUBLANE, g * p))
    return bro(a_re), bro(a_im), bcat, ccat


def _s5_body(u_ref, s0re_ref, s0im_ref, are_ref, aim_ref, bcat_ref, ccat_ref, d_ref,
             y_ref, ore_ref, oim_ref, sre, sim, st_re, st_im, *, tc, nblk, blk_state):
    c = pl.program_id(1)

    @pl.when(c == 0)
    def _():
        st_re[...] = s0re_ref[...]
        st_im[...] = s0im_ref[...]

    for j in range(nblk):
        ub = u_ref[:, j * LANE:(j + 1) * LANE].astype(BF16)
        bu = jnp.dot(ub, bcat_ref[j], preferred_element_type=F32)
        sre[:, j * blk_state:(j + 1) * blk_state] = bu[:, :blk_state]
        sim[:, j * blk_state:(j + 1) * blk_state] = bu[:, blk_state:]

    n_state = sre.shape[1]
    for strip in range(n_state // S5_STRIP):
        sl = slice(strip * S5_STRIP, (strip + 1) * S5_STRIP)
        ar = are_ref[:, sl]
        ai = aim_ref[:, sl]

        def step(t, carry, sl=sl, ar=ar, ai=ai):
            sr, si = carry
            rows = pl.ds(pl.multiple_of(t * SUBLANE, SUBLANE), SUBLANE)
            nr = ar * sr - ai * si + sre[rows, sl]
            ni = ar * si + ai * sr + sim[rows, sl]
            sre[rows, sl] = nr
            sim[rows, sl] = ni
            return nr, ni

        sr, si = lax.fori_loop(0, tc, step, (st_re[:, sl], st_im[:, sl]), unroll=4)
        st_re[:, sl] = sr
        st_im[:, sl] = si

    for j in range(nblk):
        cols = slice(j * blk_state, (j + 1) * blk_state)
        scat = jnp.concatenate([sre[:, cols].astype(BF16), sim[:, cols].astype(BF16)], axis=1)
        yj = jnp.dot(scat, ccat_ref[j], preferred_element_type=F32)
        lanes = slice(j * LANE, (j + 1) * LANE)
        yj = yj + d_ref[:, lanes] * u_ref[:, lanes]
        y_ref[:, lanes] = jax.nn.gelu(yj).astype(y_ref.dtype)

    @pl.when(c == pl.num_programs(1) - 1)
    def _():
        ore_ref[...] = st_re[...]
        oim_ref[...] = st_im[...]


def _s5(u, s0_re, s0_im, a_re, a_im, bcat, ccat, d_skip, *, n_seq, seq_len):
    w = u.shape[1]
    n_state = a_re.shape[1]
    nblk = bcat.shape[0]
    ng = n_seq // SEQ_PER_SCAN
    tc = min(seq_len, 32)
    rows = tc * SEQ_PER_SCAN
    u_tb = u.reshape(ng, SEQ_PER_SCAN, seq_len, w).transpose(0, 2, 1, 3).reshape(ng, seq_len * SEQ_PER_SCAN, w)
    s0_re = s0_re.astype(F32).reshape(ng, SEQ_PER_SCAN, n_state)
    s0_im = s0_im.astype(F32).reshape(ng, SEQ_PER_SCAN, n_state)
    state_spec = pl.BlockSpec((None, SEQ_PER_SCAN, n_state), lambda g, c: (g, 0, 0))
    y_tb, o_re, o_im = pl.pallas_call(
        functools.partial(_s5_body, tc=tc, nblk=nblk, blk_state=n_state // nblk),
        grid=(ng, seq_len // tc),
        in_specs=[
            pl.BlockSpec((None, rows, w), lambda g, c: (g, c, 0)),
            state_spec,
            state_spec,
            _resident(a_re.shape),
            _resident(a_im.shape),
            _resident(bcat.shape),
            _resident(ccat.shape),
            _resident((1, w)),
        ],
        out_specs=[
            pl.BlockSpec((None, rows, w), lambda g, c: (g, c, 0)),
            state_spec,
            state_spec,
        ],
        out_shape=[
            jax.ShapeDtypeStruct((ng, seq_len * SEQ_PER_SCAN, w), BF16),
            jax.ShapeDtypeStruct((ng, SEQ_PER_SCAN, n_state), F32),
            jax.ShapeDtypeStruct((ng, SEQ_PER_SCAN, n_state), F32),
        ],
        scratch_shapes=[
            pltpu.VMEM((rows, n_state), F32),
            pltpu.VMEM((rows, n_state), F32),
            pltpu.VMEM((SEQ_PER_SCAN, n_state), F32),
            pltpu.VMEM((SEQ_PER_SCAN, n_state), F32),
        ],
        compiler_params=_params(("parallel", "arbitrary")),
        name="s5_scan",
    )(u_tb, s0_re, s0_im, a_re, a_im, bcat, ccat, d_skip.astype(F32).reshape(1, w))
    y = y_tb.reshape(ng, seq_len, SEQ_PER_SCAN, w).transpose(0, 2, 1, 3).reshape(n_seq * seq_len, w)
    return y, o_re.reshape(n_seq, n_state), o_im.reshape(n_seq, n_state)


def _t5_bucket_table(n_q, n_k, n_prev):
    half = REL_BUCKETS // 2
    max_exact = half // 2
    assert (REL_MAX_DIST // max_exact) ** 2 == 2 ** (half - max_exact)
    rel = np.arange(n_k)[None, :] - n_prev - np.arange(n_q)[:, None]
    n = np.abs(rel)
    k = np.zeros_like(n)
    for kk in range(1, 2 * half):
        k = np.where(n * n >= max_exact * max_exact * (2 ** kk), kk, k)
    large = np.minimum(max_exact + k, half - 1)
    return ((rel > 0) * half + np.where(n < max_exact, n, large)).astype(np.int32)


def _swa_body(*refs, rows, window, n_kv, head_dim, has_past, has_prev_block):
    it = iter(refs)
    q_ref, kc_ref, vc_ref = next(it), next(it), next(it)
    kp_ref = vp_ref = kcache_ref = vcache_ref = None
    if has_prev_block:
        kp_ref, vp_ref = next(it), next(it)
    if has_past:
        kcache_ref, vcache_ref = next(it), next(it)
    bias_ref, sink_ref = next(it), next(it)
    o_ref = next(it)
    kx, vx = next(it), next(it)

    r = pl.program_id(1)
    n_k = window + CHUNK
    grp = SWA_GROUP * head_dim

    def put(dst, lo, src):
        for kv in range(n_kv):
            dst[kv, lo:lo + src.shape[0], :] = src[:, kv * head_dim:(kv + 1) * head_dim].astype(BF16)

    @pl.when(r == 0)
    def _():
        if has_past:
            put(kx, 0, kcache_ref[...])
            put(vx, 0, vcache_ref[...])
        else:
            kx[:, 0:window, :] = jnp.zeros((n_kv, window, head_dim), BF16)
            vx[:, 0:window, :] = jnp.zeros((n_kv, window, head_dim), BF16)

    if has_prev_block:
        @pl.when(r > 0)
        def _():
            put(kx, 0, kp_ref[...])
            put(vx, 0, vp_ref[...])

    put(kx, window, kc_ref[...])
    put(vx, window, vc_ref[...])

    scale = 1.0 / math.sqrt(head_dim)
    n_chunks = rows // CHUNK
    key_pos = lax.broadcasted_iota(jnp.int32, (SWA_GROUP * CHUNK, n_k), 1)

    def chunk(cc, _):
        q0 = pl.multiple_of(cc * CHUNK, CHUNK)
        qrows = pl.ds(q0, CHUNK)
        krows = pl.ds(q0, n_k)
        if not has_past:
            valid = key_pos + (r * n_chunks + cc) * CHUNK >= window
        logits = []
        for kv in range(n_kv):
            qg = q_ref[qrows, kv * grp:(kv + 1) * grp] * scale
            qs = jnp.concatenate([qg[:, g * head_dim:(g + 1) * head_dim] for g in range(SWA_GROUP)], axis=0)
            lg = lax.dot_general(qs.astype(BF16), kx[kv, krows, :], (((1,), (1,)), ((), ())),
                                 preferred_element_type=F32) + bias_ref[kv]
            if not has_past:
                lg = jnp.where(valid, lg, -1e30)
            logits.append(lg)
        probs = []
        for kv in range(n_kv):
            sink = sink_ref[kv]
            m = jnp.maximum(jnp.max(logits[kv], axis=-1, keepdims=True), sink)
            p = jnp.exp(logits[kv] - m)
            den = jnp.sum(p, axis=-1, keepdims=True) + jnp.exp(sink - m)
            probs.append((p * (1.0 / den)).astype(BF16))
        for kv in range(n_kv):
            out = jnp.dot(probs[kv], vx[kv, krows, :], preferred_element_type=F32)
            og = jnp.concatenate([out[g * CHUNK:(g + 1) * CHUNK, :] for g in range(SWA_GROUP)], axis=1)
            o_ref[qrows, kv * grp:(kv + 1) * grp] = og.astype(o_ref.dtype)
        return 0

    lax.fori_loop(0, n_chunks, chunk, 0)


def _swa(h, k_cache, v_cache, sinks, bias, *, n_seq, seq_len, q_col, k_col, v_col, q_width, kv_width, window):
    has_past = k_cache is not None
    rows = min(seq_len, 256)
    has_prev_block = seq_len > rows
    assert rows % CHUNK == 0 and (not has_prev_block or rows % window == 0)
    n_heads = sinks.shape[0]
    head_dim = q_width // n_heads
    n_kv = n_heads // SWA_GROUP
    n_k = window + CHUNK
    blocks_per_seq = seq_len // rows
    qb, kb, vb = q_col // q_width, k_col // kv_width, v_col // kv_width
    assert q_col % q_width == 0 and k_col % kv_width == 0 and v_col % kv_width == 0
    bias_g = bias.reshape(n_kv, SWA_GROUP * CHUNK, n_k)
    sink_g = jnp.repeat(sinks.astype(F32).reshape(n_kv, SWA_GROUP), CHUNK, axis=1)[..., None]

    in_specs = [
        pl.BlockSpec((rows, q_width), lambda b, r: (b * blocks_per_seq + r, qb)),
        pl.BlockSpec((rows, kv_width), lambda b, r: (b * blocks_per_seq + r, kb)),
        pl.BlockSpec((rows, kv_width), lambda b, r: (b * blocks_per_seq + r, vb)),
    ]
    args = [h, h, h]
    if has_prev_block:
        wpb = rows // window
        prev = lambda col: pl.BlockSpec(
            (window, kv_width), lambda b, r: (jnp.maximum((b * blocks_per_seq + r) * wpb - 1, 0), col))
        in_specs += [prev(kb), prev(vb)]
        args += [h, h]
    if has_past:
        cache = pl.BlockSpec((None, window, kv_width), lambda b, r: (b, 0, 0))
        in_specs += [cache, cache]
        args += [k_cache.reshape(n_seq, window, kv_width), v_cache.reshape(n_seq, window, kv_width)]
    in_specs += [_resident(bias_g.shape), _resident(sink_g.shape)]
    args += [bias_g, sink_g]

    return pl.pallas_call(
        functools.partial(_swa_body, rows=rows, window=window, n_kv=n_kv, head_dim=head_dim,
                          has_past=has_past, has_prev_block=has_prev_block),
        grid=(n_seq, blocks_per_seq),
        in_specs=in_specs,
        out_specs=pl.BlockSpec((rows, q_width), lambda b, r: (b * blocks_per_seq + r, 0)),
        out_shape=jax.ShapeDtypeStruct((n_seq * seq_len, q_width), BF16),
        scratch_shapes=[pltpu.VMEM((n_kv, window + rows, head_dim), BF16)] * 2,
        compiler_params=_params(("parallel", "arbitrary")),
        name="swa_attention",
    )(*args)


def _merge(x, y_ssm, attn, h, w_glu, w_o, w_out, *, gs_col, ga_col, tm):
    m, d = x.shape
    gcol = math.gcd(gs_col, ga_col, d)
    assert gcol % LANE == 0
    nsub = d // gcol

    def body(x_ref, y_ref, a_ref, *rest):
        gs = jnp.concatenate([r[...] for r in rest[:nsub]], axis=1)
        ga = jnp.concatenate([r[...] for r in rest[nsub:2 * nsub]], axis=1)
        wglu_ref, wo_ref, wout_ref, o_ref = rest[2 * nsub:]
        gl = jnp.dot(y_ref[...], wglu_ref[...], preferred_element_type=F32)
        ys = gl[:, :d] * jax.nn.sigmoid(gl[:, d:])
        ya = jnp.dot(a_ref[...], wo_ref[...], preferred_element_type=F32)
        merged = jax.nn.sigmoid(gs) * ys + jax.nn.sigmoid(ga) * ya
        o_ref[...] = x_ref[...] + jnp.dot(merged.astype(BF16), wout_ref[...], preferred_element_type=F32)

    gate_specs = [pl.BlockSpec((tm, gcol), lambda i, c=c: (i, c))
                  for base in (gs_col, ga_col) for c in range(base // gcol, base // gcol + nsub)]
    return pl.pallas_call(
        body,
        grid=(m // tm,),
        in_specs=[
            pl.BlockSpec((tm, d), lambda i: (i, 0)),
            pl.BlockSpec((tm, y_ssm.shape[1]), lambda i: (i, 0)),
            pl.BlockSpec((tm, attn.shape[1]), lambda i: (i, 0)),
            *gate_specs,
            _resident(w_glu.shape),
            _resident(w_o.shape),
            _resident(w_out.shape),
        ],
        out_specs=pl.BlockSpec((tm, d), lambda i: (i, 0)),
        out_shape=jax.ShapeDtypeStruct((m, d), F32),
        compiler_params=_params(("parallel",)),
        name="merge_out_proj",
    )(x, y_ssm, attn, *([h] * (2 * nsub)), w_glu, w_o, w_out)


def _xattn_body(x_ref, g_ref, mk_ref, mv_ref, wq_ref, wo_ref, o_ref, *, n_heads, head_dim):
    x = x_ref[...]
    q = jnp.dot(_rms(x, g_ref[...]).astype(BF16), wq_ref[...], preferred_element_type=F32).astype(BF16)
    scale = 1.0 / math.sqrt(head_dim)
    outs = []
    for hd in range(n_heads):
        hs = slice(hd * head_dim, (hd + 1) * head_dim)
        logits = lax.dot_general(q[:, hs], mk_ref[:, hs].astype(BF16), (((1,), (1,)), ((), ())),
                                 preferred_element_type=F32) * scale
        m = jnp.max(logits, axis=-1, keepdims=True)
        p = jnp.exp(logits - m)
        probs = p / jnp.sum(p, axis=-1, keepdims=True)
        outs.append(jnp.dot(probs.astype(BF16), mv_ref[:, hs].astype(BF16), preferred_element_type=F32))
    att = jnp.concatenate(outs, axis=1).astype(BF16)
    o_ref[...] = x + jnp.dot(att, wo_ref[...], preferred_element_type=F32)


def _xattn(x, g, mk, mv, w_q, w_o, *, n_seq, seq_len, n_heads):
    m, d = x.shape
    tm = min(seq_len, 256)
    blocks_per_seq = seq_len // tm
    mem_len, xw = mk.shape[1], mk.shape[2]
    return pl.pallas_call(
        functools.partial(_xattn_body, n_heads=n_heads, head_dim=xw // n_heads),
        grid=(n_seq, blocks_per_seq),
        in_specs=[
            pl.BlockSpec((tm, d), lambda b, r: (b * blocks_per_seq + r, 0)),
            _resident((1, d)),
            pl.BlockSpec((None, mem_len, xw), lambda b, r: (b, 0, 0)),
            pl.BlockSpec((None, mem_len, xw), lambda b, r: (b, 0, 0)),
            _resident(w_q.shape),
            _resident(w_o.shape),
        ],
        out_specs=pl.BlockSpec((tm, d), lambda b, r: (b * blocks_per_seq + r, 0)),
        out_shape=jax.ShapeDtypeStruct((m, d), F32),
        compiler_params=_params(("parallel", "arbitrary")),
        name="cross_attention",
    )(x, g.reshape(1, d), mk, mv, w_q, w_o)


def _router_body(xp_ref, xs_ref, g_ref, wr_ref, br_ref, xn_ref, idx_ref, gate_ref, rank_ref, cnt_ref, carry,
                 *, n_exp, n_prompt_tiles):
    i = pl.program_id(0)

    @pl.when(i == 0)
    def _():
        carry[...] = jnp.zeros_like(carry)

    xn = _rms(jnp.where(i < n_prompt_tiles, xp_ref[...], xs_ref[...]), g_ref[...])
    xn_ref[...] = xn
    tm = xn.shape[0]
    logits = lax.dot_general(wr_ref[...], xn, (((1,), (1,)), ((), ())),
                             precision=lax.Precision.HIGHEST, preferred_element_type=F32) + br_ref[...]
    eid = lax.broadcasted_iota(jnp.int32, (n_exp, tm), 0)
    vals, hots = [], []
    work = logits
    for k in range(TOP_K):
        m = jnp.max(work, axis=0, keepdims=True)
        sel = jnp.min(jnp.where(work == m, eid, n_exp), axis=0, keepdims=True)
        hot = eid == sel
        vals.append(m)
        hots.append(hot)
        idx_ref[k:k + 1, :] = sel
        work = jnp.where(hot, -jnp.inf, work)
    ex = [jnp.exp(v - vals[0]) for v in vals]
    tot = ex[0] + ex[1] + ex[2] + ex[3]
    for k in range(TOP_K):
        gate_ref[k:k + 1, :] = ex[k] / tot

    chosen = (hots[0] | hots[1] | hots[2] | hots[3])
    a = jnp.where(chosen, 1.0, 0.0)
    earlier = (lax.broadcasted_iota(jnp.int32, (tm, tm), 0) < lax.broadcasted_iota(jnp.int32, (tm, tm), 1))
    prefix = jnp.dot(a.astype(BF16), jnp.where(earlier, 1.0, 0.0).astype(BF16),
                     preferred_element_type=F32) + carry[...]
    for k in range(TOP_K):
        rank_ref[k:k + 1, :] = jnp.sum(jnp.where(hots[k], prefix, 0.0), axis=0, keepdims=True).astype(jnp.int32)
    carry[...] = carry[...] + jnp.sum(a, axis=1, keepdims=True)
    cnt_ref[...] = carry[...].astype(jnp.int32)


def _two_group_specs(tm, d, npt):
    return [pl.BlockSpec((tm, d), lambda i: (jnp.minimum(i, npt - 1), 0)),
            pl.BlockSpec((tm, d), lambda i: (jnp.maximum(i - npt, 0), 0))]


def _router(x_p, x_s, g, w_router, b_router):
    d = x_p.shape[1]
    t = x_p.shape[0] + x_s.shape[0]
    n_exp = w_router.shape[1]
    tm = ROUTER_TM
    assert x_p.shape[0] % tm == 0 and x_s.shape[0] % tm == 0
    npt = x_p.shape[0] // tm
    row = pl.BlockSpec((TOP_K, tm), lambda i: (0, i))
    return pl.pallas_call(
        functools.partial(_router_body, n_exp=n_exp, n_prompt_tiles=npt),
        grid=(t // tm,),
        in_specs=[
            *_two_group_specs(tm, d, npt),
            _resident((1, d)),
            _resident((n_exp, d)),
            _resident((n_exp, 1)),
        ],
        out_specs=[pl.BlockSpec((tm, d), lambda i: (i, 0)), row, row, row,
                   pl.BlockSpec((n_exp, 1), lambda i: (0, 0))],
        out_shape=[
            jax.ShapeDtypeStruct((t, d), F32),
            jax.ShapeDtypeStruct((TOP_K, t), jnp.int32),
            jax.ShapeDtypeStruct((TOP_K, t), F32),
            jax.ShapeDtypeStruct((TOP_K, t), jnp.int32),
            jax.ShapeDtypeStruct((n_exp, 1), jnp.int32),
        ],
        scratch_shapes=[pltpu.VMEM((n_exp, 1), F32)],
        compiler_params=_params(("arbitrary",)),
        name="moe_router",
    )(x_p, x_s, g.reshape(1, d), w_router.astype(F32).T, b_router.astype(F32).reshape(n_exp, 1))


def _dispatch_body(pos_ref, fill_ref, x_ref, xs_ref, zeros, sem, fill_sem, *, n_exp):
    tm, d = x_ref.shape
    fill_rows = zeros.shape[0]

    def fill_copy(e):
        start = pl.multiple_of(fill_ref[e], SUBLANE)
        return pltpu.make_async_copy(zeros, xs_ref.at[pl.ds(start, fill_rows), :], fill_sem)

    @pl.when(pl.program_id(0) == 0)
    def _():
        zeros[...] = jnp.zeros_like(zeros)
        for e in range(n_exp):
            fill_copy(e).start()
        for e in range(n_exp):
            fill_copy(e).wait()

    def issue(t, _):
        for k in range(TOP_K):
            pltpu.make_async_copy(x_ref.at[pl.ds(t, 1), :], xs_ref.at[pl.ds(pos_ref[k, t], 1), :], sem).start()
        return 0

    lax.fori_loop(0, tm, issue, 0, unroll=8)
    for k in range(TOP_K):
        pltpu.make_async_copy(x_ref, xs_ref.at[pl.ds(0, tm), :], sem).wait()


def _dispatch(xn, pos, fill_start, n_rows):
    t, d = xn.shape
    tm = TOK_TM
    n_exp = fill_start.shape[0]
    fill_rows = MOE_SB + SUBLANE
    return pl.pallas_call(
        functools.partial(_dispatch_body, n_exp=n_exp),
        grid=(t // tm,),
        in_specs=[
            pl.BlockSpec((TOP_K, tm), lambda i: (0, i), memory_space=pltpu.SMEM),
            pl.BlockSpec(memory_space=pltpu.SMEM),
            pl.BlockSpec((tm, d), lambda i: (i, 0)),
        ],
        out_specs=pl.BlockSpec(memory_space=pl.ANY),
        out_shape=jax.ShapeDtypeStruct((n_rows + fill_rows, d), xn.dtype),
        scratch_shapes=[pltpu.VMEM((fill_rows, d), xn.dtype), pltpu.SemaphoreType.DMA, pltpu.SemaphoreType.DMA],
        compiler_params=_params(("arbitrary",)),
        name="moe_dispatch",
    )(pos, fill_start, xn)


def _experts_body(te_ref, row0_ref, nv_ref, xs_ref, wg_ref, wl_ref, bg_ref, bl_ref, wd_ref, bd_ref,
                  ys_ref, acc, xb, stage, wgb, wlb, wdb, in_sem, out_sem, *, n_j):
    del te_ref
    i, j = pl.program_id(0), pl.program_id(1)
    n_valid = nv_ref[i]
    row0 = row0_ref[i]
    d = acc.shape[1]
    n_sub = (n_valid + MOE_SB - 1) // MOE_SB
    n_big = n_valid // MOE_BIG
    small0 = n_big * MOE_BIG
    n_small = n_sub - n_big * (MOE_BIG // MOE_SB)

    def in_copy(s, slot):
        src = pl.ds(pl.multiple_of(row0 + s * MOE_SB, MOE_SB), MOE_SB)
        return pltpu.make_async_copy(xs_ref.at[src, :], stage.at[slot], in_sem.at[slot])

    def out_copy(first, size):
        src = pl.ds(pl.multiple_of(first, MOE_SB), size)
        dst = pl.ds(pl.multiple_of(row0 + first, MOE_SB), size)
        return pltpu.make_async_copy(acc.at[src, :], ys_ref.at[dst, :], out_sem)

    def compute(first, size):
        rows = pl.ds(pl.multiple_of(first, MOE_SB), size)
        xs = xb[rows, :]
        hg = jnp.dot(xs, wgb[...], preferred_element_type=F32) + bg_ref[...]
        hl = jnp.dot(xs, wlb[...], preferred_element_type=F32) + bl_ref[...]
        glu = jnp.minimum(hg, SWIGLU_LIMIT)
        lin = jnp.clip(hl, -SWIGLU_LIMIT, SWIGLU_LIMIT)
        act = (glu * jax.nn.sigmoid(SWIGLU_ALPHA * glu) * (lin + 1.0)).astype(BF16)
        for c in range(0, d, MOE_DOWN_COLS):
            cols = slice(c, c + MOE_DOWN_COLS)
            acc[rows, cols] += jnp.dot(act, wdb[:, cols], preferred_element_type=F32)

        @pl.when(j == n_j - 1)
        def _():
            out_copy(first, size).start()

    @pl.when(n_valid > 0)
    def _():
        @pl.when(j == 0)
        def _():
            in_copy(0, 0).start()

            def load(s, _):
                slot = s % 2
                rows = pl.ds(pl.multiple_of(s * MOE_SB, MOE_SB), MOE_SB)

                @pl.when(s + 1 < n_sub)
                def _():
                    in_copy(s + 1, 1 - slot).start()

                in_copy(s, slot).wait()
                xb[rows, :] = stage[slot].astype(BF16)
                acc[rows, :] = jnp.broadcast_to(bd_ref[...], (MOE_SB, d))
                return 0

            lax.fori_loop(0, n_sub, load, 0)

        wgb[...] = wg_ref[...].astype(BF16)
        wlb[...] = wl_ref[...].astype(BF16)
        wdb[...] = wd_ref[...].astype(BF16)

        def big(s, _):
            compute(s * MOE_BIG, MOE_BIG)
            return 0

        def small(s, _):
            compute(small0 + s * MOE_SB, MOE_SB)
            return 0

        lax.fori_loop(0, n_big, big, 0)
        lax.fori_loop(0, n_small, small, 0)

        @pl.when(j == n_j - 1)
        def _():
            def drain_big(s, _):
                out_copy(s * MOE_BIG, MOE_BIG).wait()
                return 0

            def drain_small(s, _):
                out_copy(small0 + s * MOE_SB, MOE_SB).wait()
                return 0

            lax.fori_loop(0, n_big, drain_big, 0)
            lax.fori_loop(0, n_small, drain_small, 0)


def _experts(xs, tile_expert, tile_row0, tile_valid, n_rows, w_up, b_up, w_down, b_down):
    d = xs.shape[1]
    n_tiles = tile_expert.shape[0]
    depth, n_exp, _, ff2 = w_up.shape
    assert depth == 1
    d_ff = ff2 // 2
    n_j = d_ff // MOE_TF
    b_up = b_up.reshape(depth, n_exp, 1, ff2)
    b_down = b_down.reshape(depth, n_exp, 1, d)

    def jsel(i, j, nv):
        return jnp.where(nv[i] > 0, j, n_j - 1)

    grid_spec = pltpu.PrefetchScalarGridSpec(
        num_scalar_prefetch=3,
        grid=(n_tiles, n_j),
        in_specs=[
            pl.BlockSpec(memory_space=pl.ANY),
            pl.BlockSpec((None, None, d, MOE_TF), lambda i, j, te, r0, nv: (0, te[i], 0, jsel(i, j, nv))),
            pl.BlockSpec((None, None, d, MOE_TF), lambda i, j, te, r0, nv: (0, te[i], 0, n_j + jsel(i, j, nv))),
            pl.BlockSpec((None, None, 1, MOE_TF), lambda i, j, te, r0, nv: (0, te[i], 0, jsel(i, j, nv))),
            pl.BlockSpec((None, None, 1, MOE_TF), lambda i, j, te, r0, nv: (0, te[i], 0, n_j + jsel(i, j, nv))),
            pl.BlockSpec((None, None, MOE_TF, d), lambda i, j, te, r0, nv: (0, te[i], jsel(i, j, nv), 0)),
            pl.BlockSpec((None, None, 1, d), lambda i, j, te, r0, nv: (0, te[i], 0, 0)),
        ],
        out_specs=pl.BlockSpec(memory_space=pl.ANY),
        scratch_shapes=[
            pltpu.VMEM((MOE_CAP, d), F32),
            pltpu.VMEM((MOE_CAP, d), BF16),
            pltpu.VMEM((2, MOE_SB, d), F32),
            pltpu.VMEM((d, MOE_TF), BF16),
            pltpu.VMEM((d, MOE_TF), BF16),
            pltpu.VMEM((MOE_TF, d), BF16),
            pltpu.SemaphoreType.DMA((2,)),
            pltpu.SemaphoreType.DMA,
        ],
    )
    return pl.pallas_call(
        functools.partial(_experts_body, n_j=n_j),
        grid_spec=grid_spec,
        out_shape=jax.ShapeDtypeStruct((n_rows, d), F32),
        compiler_params=_params(("arbitrary", "arbitrary")),
        name="moe_experts",
    )(tile_expert, tile_row0, tile_valid, xs, w_up, w_up, b_up, b_up, w_down, b_down)


def _combine_body(pos_ref, nxt_ref, xp_ref, xs_ref, gate_ref, g_ref, ys_ref, op_ref, os_ref, buf, sem,
                  *, n_prompt_tiles):
    i = pl.program_id(0)
    n = pl.num_programs(0)
    tm = xp_ref.shape[0]

    def gather(p_ref, slot):
        def issue(t, _):
            for k in range(TOP_K):
                pltpu.make_async_copy(ys_ref.at[pl.ds(p_ref[k, t], 1), :], buf.at[slot, k, pl.ds(t, 1), :],
                                      sem.at[slot]).start()
            return 0

        lax.fori_loop(0, tm, issue, 0, unroll=8)

    slot = i % 2

    @pl.when(i == 0)
    def _():
        gather(pos_ref, 0)

    @pl.when(i + 1 < n)
    def _():
        gather(nxt_ref, 1 - slot)

    for k in range(TOP_K):
        pltpu.make_async_copy(ys_ref.at[pl.ds(0, tm), :], buf.at[slot, k], sem.at[slot]).wait()

    out = jnp.where(i < n_prompt_tiles, xp_ref[...], xs_ref[...])
    for k in range(TOP_K):
        out = out + gate_ref[:, k:k + 1] * buf[slot, k]
    y = _rms(out, g_ref[...])

    @pl.when(i < n_prompt_tiles)
    def _():
        op_ref[...] = y

    @pl.when(i >= n_prompt_tiles)
    def _():
        os_ref[...] = y


def _combine(x_p, x_s, ys, pos, gate_t, g_final):
    n_prompt, d = x_p.shape
    t = n_prompt + x_s.shape[0]
    tm = TOK_TM
    assert n_prompt % tm == 0 and t % tm == 0
    npt = n_prompt // tm
    n = t // tm
    return pl.pallas_call(
        functools.partial(_combine_body, n_prompt_tiles=npt),
        grid=(n,),
        in_specs=[
            pl.BlockSpec((TOP_K, tm), lambda i: (0, i), memory_space=pltpu.SMEM),
            pl.BlockSpec((TOP_K, tm), lambda i: (0, jnp.minimum(i + 1, n - 1)), memory_space=pltpu.SMEM),
            *_two_group_specs(tm, d, npt),
            pl.BlockSpec((tm, TOP_K), lambda i: (i, 0)),
            _resident((1, d)),
            pl.BlockSpec(memory_space=pl.ANY),
        ],
        out_specs=_two_group_specs(tm, d, npt),
        out_shape=[
            jax.ShapeDtypeStruct((n_prompt, d), F32),
            jax.ShapeDtypeStruct((t - n_prompt, d), F32),
        ],
        scratch_shapes=[pltpu.VMEM((2, TOP_K, tm, d), F32), pltpu.SemaphoreType.DMA((2,))],
        compiler_params=_params(("arbitrary",)),
        name="moe_combine",
    )(pos, pos, x_p, x_s, gate_t, g_final.reshape(1, d), ys)


def _moe_layout(idx, rank, counts, n_tiles):
    n_exp = counts.shape[0]
    region = ((counts + MOE_SB - 1) // MOE_SB) * MOE_SB
    row_start = jnp.cumsum(region) - region
    tiles_per_exp = (counts + MOE_CAP - 1) // MOE_CAP
    tile_end = jnp.cumsum(tiles_per_exp)
    tile_start = tile_end - tiles_per_exp
    n_used = tile_end[-1]
    tiles = jnp.arange(n_tiles, dtype=jnp.int32)
    te = jnp.minimum(jnp.searchsorted(tile_end, tiles, side="right"), n_exp - 1).astype(jnp.int32)
    te = jnp.where(tiles < n_used, te, te[jnp.maximum(n_used - 1, 0)])
    first = (tiles - tile_start[te]) * MOE_CAP
    valid = jnp.where(tiles < n_used, jnp.clip(counts[te] - first, 0, MOE_CAP), 0)
    row0 = jnp.where(tiles < n_used, row_start[te] + first, 0)
    hot = idx[..., None] == jnp.arange(n_exp, dtype=jnp.int32)
    pos = rank + jnp.sum(jnp.where(hot, row_start, 0), axis=-1)
    fill_start = row_start + (counts // SUBLANE) * SUBLANE
    i32 = lambda a: a.astype(jnp.int32)
    return i32(pos), te, i32(row0), i32(valid), i32(fill_start)


def kernel(x_prompt, x_sample, state_ssm_re, state_ssm_im, cache_swa_k, cache_swa_v, cache_mem_k, cache_mem_v, mem_prompt, w_in, ssm_lam_re, ssm_lam_im, ssm_log_dt, ssm_b_re, ssm_b_im, ssm_c_re, ssm_c_im, ssm_d, w_glu, swa_sinks, rel_bias, w_o_attn, w_out, w_xq, w_mk, w_mv, w_xo, w_router, b_router, w_up, b_up, w_down, b_down, norm_mix, norm_x, norm_mem, norm_moe, norm_final):
    depth = w_in.shape[0]
    assert depth == 1, "single-layer step"
    bp, lp, d = x_prompt.shape
    bs, ls, _ = x_sample.shape
    n_p, n_s = bp * lp, bs * ls
    t = n_p + n_s
    _, _, window, kvh, hd = cache_swa_k.shape
    kv_width = kvh * hd
    n_heads = swa_sinks.shape[1]
    q_width = n_heads * hd
    ssm_width = ssm_d.shape[1]
    _, _, mem_len, x_heads, x_hd = cache_mem_k.shape
    xw = x_heads * x_hd
    ssm_groups, ssm_state = ssm_lam_re.shape[1], ssm_lam_re.shape[2]
    q_col = ssm_width
    k_col = q_col + q_width
    v_col = k_col + kv_width
    gs_col = v_col + kv_width
    ga_col = gs_col + d

    l = 0
    w_in_b = w_in[l].astype(BF16)
    w_glu_b, w_o_b, w_out_b = w_glu[l].astype(BF16), w_o_attn[l].astype(BF16), w_out[l].astype(BF16)
    w_xq_b, w_xo_b = w_xq[l].astype(BF16), w_xo[l].astype(BF16)
    w_mkv_b = jnp.concatenate([w_mk[l], w_mv[l]], axis=1).astype(BF16)
    a_re, a_im, bcat, ccat = _s5_discretize(ssm_lam_re[l], ssm_lam_im[l], ssm_log_dt[l], ssm_b_re[l],
                                            ssm_b_im[l], ssm_c_re[l], ssm_c_im[l])
    bucket = _t5_bucket_table(CHUNK, window + CHUNK, window)
    bias = jnp.transpose(rel_bias.astype(F32)[bucket], (2, 0, 1))

    def mixers(x, n_seq, seq_len, s0_re, s0_im, k_cache, v_cache):
        xf = x.reshape(n_seq * seq_len, d)
        h = _norm_matmul(xf, norm_mix[l], w_in_b, tm=1024, tn=512)
        y_ssm, s_re, s_im = _s5(h[:, :ssm_width], s0_re, s0_im, a_re, a_im, bcat, ccat, ssm_d[l],
                                n_seq=n_seq, seq_len=seq_len)
        attn = _swa(h, k_cache, v_cache, swa_sinks[l], bias, n_seq=n_seq, seq_len=seq_len, q_col=q_col,
                    k_col=k_col, v_col=v_col, q_width=q_width, kv_width=kv_width, window=window)
        x1 = _merge(xf, y_ssm, attn, h, w_glu_b, w_o_b, w_out_b, gs_col=gs_col, ga_col=ga_col, tm=256)
        h3 = h.reshape(n_seq, seq_len, -1)
        k_new = h3[:, :, k_col:k_col + kv_width]
        v_new = h3[:, :, v_col:v_col + kv_width]
        if k_cache is not None:
            k_new = jnp.concatenate([k_cache.reshape(n_seq, window, kv_width), k_new], axis=1)
            v_new = jnp.concatenate([v_cache.reshape(n_seq, window, kv_width), v_new], axis=1)
        k_new = k_new[:, -window:].reshape(1, n_seq, window, kvh, hd)
        v_new = v_new[:, -window:].reshape(1, n_seq, window, kvh, hd)
        state_shape = (1, n_seq, ssm_groups, ssm_state)
        return x1, s_re.reshape(state_shape), s_im.reshape(state_shape), k_new, v_new

    zero_state = jnp.zeros((bp, ssm_groups * ssm_state), F32)
    x1_p, sre_p, sim_p, k_p, v_p = mixers(x_prompt, bp, lp, zero_state, zero_state, None, None)
    x1_s, sre_s, sim_s, k_s, v_s = mixers(x_sample, bs, ls, state_ssm_re[l], state_ssm_im[l],
                                          cache_swa_k[l], cache_swa_v[l])

    mkv = _norm_matmul(mem_prompt.reshape(bp * mem_len, d), norm_mem[l], w_mkv_b, tm=1024, tn=512)
    mkv = mkv.reshape(bp, mem_len, 2 * xw)
    mk_p, mv_p = mkv[:, :, :xw], mkv[:, :, xw:]

    x2_p = _xattn(x1_p, norm_x[l], mk_p, mv_p, w_xq_b, w_xo_b, n_seq=bp, seq_len=lp, n_heads=x_heads)
    x2_s = _xattn(x1_s, norm_x[l], cache_mem_k[l].reshape(bs, mem_len, xw), cache_mem_v[l].reshape(bs, mem_len, xw),
                  w_xq_b, w_xo_b, n_seq=bs, seq_len=ls, n_heads=x_heads)

    n_exp = w_router.shape[2]
    xn, idx, gate, rank, counts = _router(x2_p, x2_s, norm_moe[l], w_router[l], b_router[l])
    n_tiles = (t * TOP_K) // MOE_CAP + n_exp
    n_rows = t * TOP_K + n_exp * MOE_SB
    pos, tile_expert, tile_row0, tile_valid, fill_start = _moe_layout(idx, rank, counts[:, 0], n_tiles)
    xs = _dispatch(xn, pos, fill_start, n_rows)
    ys = _experts(xs, tile_expert, tile_row0, tile_valid, n_rows, w_up, b_up, w_down, b_down)
    y_p, y_s = _combine(x2_p, x2_s, ys, pos, gate.T, norm_final)

    return (y_p.reshape(bp, lp, d), y_s.reshape(bs, ls, d), sre_p, sim_p, k_p, v_p,
            mk_p.reshape(1, bp, mem_len, x_heads, x_hd), mv_p.reshape(1, bp, mem_len, x_heads, x_hd),
            sre_s, sim_s, k_s, v_s)
```

```python
import functools
import math

import numpy as np
import jax
import jax.numpy as jnp
from jax import lax
from jax.experimental import pallas as pl
from jax.experimental.pallas import tpu as pltpu

F32 = jnp.float32
BF16 = jnp.bfloat16

LANE = 128
SUBLANE = 8
VMEM_LIMIT = 56 * 1024 * 1024

CHUNK = 64
SSM_GROUP = 16
SWA_GROUP = 4
REL_BUCKETS = 32
REL_MAX_DIST = 128
TOP_K = 4
SWIGLU_LIMIT = 7.0
SWIGLU_ALPHA = 1.702
RMS_EPS = 1e-5
SEQ_PER_SCAN = SUBLANE

S5_STRIP = 512
MOE_CAP = 2560
MOE_SB = 256
MOE_BIG = 1024
MOE_DOWN_COLS = 512
MOE_TF = 256
TOK_TM = 256
ROUTER_TM = 512


def _params(sem, vmem=VMEM_LIMIT):
    return pltpu.CompilerParams(dimension_semantics=sem, vmem_limit_bytes=vmem)


def _rms(x, g):
    return x * lax.rsqrt(jnp.mean(x * x, axis=-1, keepdims=True) + RMS_EPS) * g


def _resident(shape):
    nd = len(shape)
    return pl.BlockSpec(shape, lambda *_: (0,) * nd, pipeline_mode=pl.Buffered(1))


def _norm_matmul_body(x_ref, g_ref, w_ref, o_ref, xn_ref):
    @pl.when(pl.program_id(1) == 0)
    def _():
        xn_ref[...] = _rms(x_ref[...], g_ref[...]).astype(BF16)

    o_ref[...] = jnp.dot(xn_ref[...], w_ref[...], preferred_element_type=F32)


def _norm_matmul(x, g, w, *, tm, tn):
    m, d = x.shape
    n = w.shape[1]
    tm = min(tm, m)
    assert m % tm == 0 and n % tn == 0
    return pl.pallas_call(
        _norm_matmul_body,
        grid=(m // tm, n // tn),
        in_specs=[
            pl.BlockSpec((tm, d), lambda i, j: (i, 0)),
            pl.BlockSpec((1, d), lambda i, j: (0, 0)),
            pl.BlockSpec((d, tn), lambda i, j: (0, j)),
        ],
        out_specs=pl.BlockSpec((tm, tn), lambda i, j: (i, j)),
        out_shape=jax.ShapeDtypeStruct((m, n), F32),
        scratch_shapes=[pltpu.VMEM((tm, d), BF16)],
        compiler_params=_params(("parallel", "arbitrary")),
        name="norm_matmul",
    )(x, g.reshape(1, d), w)


def _s5_discretize(lam_re, lam_im, log_dt, b_re, b_im, c_re, c_im):
    g, p = lam_re.shape
    gpb = LANE // SSM_GROUP
    nblk = g // gpb
    lr, li = lam_re.astype(F32), lam_im.astype(F32)
    dt = jnp.exp(log_dt.astype(F32))[:, None]
    mag = jnp.exp(lr * dt)
    a_re = mag * jnp.cos(li * dt)
    a_im = mag * jnp.sin(li * dt)
    den = lr * lr + li * li
    f_re = ((a_re - 1.0) * lr + a_im * li) / den
    f_im = (a_im * lr - (a_re - 1.0) * li) / den
    br, bi = b_re.astype(F32), b_im.astype(F32)
    bb_re = f_re[..., None] * br - f_im[..., None] * bi
    bb_im = f_re[..., None] * bi + f_im[..., None] * br
    eye = jnp.eye(gpb, dtype=F32)

    def b_blocks(bb):
        t = jnp.transpose(bb, (0, 2, 1)).reshape(nblk, gpb, SSM_GROUP, p)
        return jnp.einsum("jghp,gk->jghkp", t, eye).reshape(nblk, gpb * SSM_GROUP, gpb * p)

    def c_blocks(c):
        t = c.astype(F32).reshape(nblk, gpb, SSM_GROUP, p)
        return jnp.einsum("jghp,gk->jgpkh", t, eye).reshape(nblk, gpb * p, gpb * SSM_GROUP)

    bcat = jnp.concatenate([b_blocks(bb_re), b_blocks(bb_im)], axis=2).astype(BF16)
    ccat = jnp.concatenate([c_blocks(c_re), -c_blocks(c_im)], axis=1).astype(BF16)
    bro = lambda a: jnp.broadcast_to(a.reshape(1, g * p), (SUBLANE, g * p))
    return bro(a_re), bro(a_im), bcat, ccat


def _s5_body(u_ref, s0re_ref, s0im_ref, are_ref, aim_ref, bcat_ref, ccat_ref, d_ref,
             y_ref, ore_ref, oim_ref, sre, sim, st_re, st_im, *, tc, nblk, blk_state):
    c = pl.program_id(1)

    @pl.when(c == 0)
    def _():
        st_re[...] = s0re_ref[...]
        st_im[...] = s0im_ref[...]

    for j in range(nblk):
        ub = u_ref[:, j * LANE:(j + 1) * LANE].astype(BF16)
        bu = jnp.dot(ub, bcat_ref[j], preferred_element_type=F32)
        sre[:, j * blk_state:(j + 1) * blk_state] = bu[:, :blk_state]
        sim[:, j * blk_state:(j + 1) * blk_state] = bu[:, blk_state:]

    n_state = sre.shape[1]
    for strip in range(n_state // S5_STRIP):
        sl = slice(strip * S5_STRIP, (strip + 1) * S5_STRIP)
        ar = are_ref[:, sl]
        ai = aim_ref[:, sl]

        def step(t, carry, sl=sl, ar=ar, ai=ai):
            sr, si = carry
            rows = pl.ds(pl.multiple_of(t * SUBLANE, SUBLANE), SUBLANE)
            nr = ar * sr - ai * si + sre[rows, sl]
            ni = ar * si + ai * sr + sim[rows, sl]
            sre[rows, sl] = nr
            sim[rows, sl] = ni
            return nr, ni

        sr, si = lax.fori_loop(0, tc, step, (st_re[:, sl], st_im[:, sl]), unroll=4)
        st_re[:, sl] = sr
        st_im[:, sl] = si

    for j in range(nblk):
        cols = slice(j * blk_state, (j + 1) * blk_state)
        scat = jnp.concatenate([sre[:, cols].astype(BF16), sim[:, cols].astype(BF16)], axis=1)
        yj = jnp.dot(scat, ccat_ref[j], preferred_element_type=F32)
        lanes = slice(j * LANE, (j + 1) * LANE)
        yj = yj + d_ref[:, lanes] * u_ref[:, lanes]
        y_ref[:, lanes] = jax.nn.gelu(yj).astype(y_ref.dtype)

    @pl.when(c == pl.num_programs(1) - 1)
    def _():
        ore_ref[...] = st_re[...]
        oim_ref[...] = st_im[...]


def _s5(u, s0_re, s0_im, a_re, a_im, bcat, ccat, d_skip, *, n_seq, seq_len):
    w = u.shape[1]
    n_state = a_re.shape[1]
    nblk = bcat.shape[0]
    ng = n_seq // SEQ_PER_SCAN
    tc = min(seq_len, 32)
    rows = tc * SEQ_PER_SCAN
    u_tb = u.reshape(ng, SEQ_PER_SCAN, seq_len, w).transpose(0, 2, 1, 3).reshape(ng, seq_len * SEQ_PER_SCAN, w)
    s0_re = s0_re.astype(F32).reshape(ng, SEQ_PER_SCAN, n_state)
    s0_im = s0_im.astype(F32).reshape(ng, SEQ_PER_SCAN, n_state)
    state_spec = pl.BlockSpec((None, SEQ_PER_SCAN, n_state), lambda g, c: (g, 0, 0))
    y_tb, o_re, o_im = pl.pallas_call(
        functools.partial(_s5_body, tc=tc, nblk=nblk, blk_state=n_state // nblk),
        grid=(ng, seq_len // tc),
        in_specs=[
            pl.BlockSpec((None, rows, w), lambda g, c: (g, c, 0)),
            state_spec,
            state_spec,
            _resident(a_re.shape),
            _resident(a_im.shape),
            _resident(bcat.shape),
            _resident(ccat.shape),
            _resident((1, w)),
        ],
        out_specs=[
            pl.BlockSpec((None, rows, w), lambda g, c: (g, c, 0)),
            state_spec,
            state_spec,
        ],
        out_shape=[
            jax.ShapeDtypeStruct((ng, seq_len * SEQ_PER_SCAN, w), BF16),
            jax.ShapeDtypeStruct((ng, SEQ_PER_SCAN, n_state), F32),
            jax.ShapeDtypeStruct((ng, SEQ_PER_SCAN, n_state), F32),
        ],
        scratch_shapes=[
            pltpu.VMEM((rows, n_state), F32),
            pltpu.VMEM((rows, n_state), F32),
            pltpu.VMEM((SEQ_PER_SCAN, n_state), F32),
            pltpu.VMEM((SEQ_PER_SCAN, n_state), F32),
        ],
        compiler_params=_params(("parallel", "arbitrary")),
        name="s5_scan",
    )(u_tb, s0_re, s0_im, a_re, a_im, bcat, ccat, d_skip.astype(F32).reshape(1, w))
    y = y_tb.reshape(ng, seq_len, SEQ_PER_SCAN, w).transpose(0, 2, 1, 3).reshape(n_seq * seq_len, w)
    return y, o_re.reshape(n_seq, n_state), o_im.reshape(n_seq, n_state)


def _t5_bucket_table(n_q, n_k, n_prev):
    half = REL_BUCKETS // 2
    max_exact = half // 2
    assert (REL_MAX_DIST // max_exact) ** 2 == 2 ** (half - max_exact)
    rel = np.arange(n_k)[None, :] - n_prev - np.arange(n_q)[:, None]
    n = np.abs(rel)
    k = np.zeros_like(n)
    for kk in range(1, 2 * half):
        k = np.where(n * n >= max_exact * max_exact * (2 ** kk), kk, k)
    large = np.minimum(max_exact + k, half - 1)
    return ((rel > 0) * half + np.where(n < max_exact, n, large)).astype(np.int32)


def _swa_body(*refs, rows, window, n_kv, head_dim, has_past, has_prev_block):
    it = iter(refs)
    q_ref, kc_ref, vc_ref = next(it), next(it), next(it)
    kp_ref = vp_ref = kcache_ref = vcache_ref = None
    if has_prev_block:
        kp_ref, vp_ref = next(it), next(it)
    if has_past:
        kcache_ref, vcache_ref = next(it), next(it)
    bias_ref, sink_ref = next(it), next(it)
    o_ref = next(it)
    kx, vx = next(it), next(it)

    r = pl.program_id(1)
    n_k = window + CHUNK
    pair = 2 * head_dim

    def put(dst, lo, src):
        n = src.shape[0]
        zero = jnp.zeros((n, head_dim), F32)
        for kv in range(n_kv):
            x = src[:, kv * head_dim:(kv + 1) * head_dim]
            dst[0, kv, lo:lo + n, :] = jnp.concatenate([x, zero], axis=1).astype(BF16)
            dst[1, kv, lo:lo + n, :] = jnp.concatenate([zero, x], axis=1).astype(BF16)

    @pl.when(r == 0)
    def _():
        if has_past:
            put(kx, 0, kcache_ref[...])
            put(vx, 0, vcache_ref[...])
        else:
            kx[:, :, 0:window, :] = jnp.zeros((2, n_kv, window, pair), BF16)
            vx[:, :, 0:window, :] = jnp.zeros((2, n_kv, window, pair), BF16)

    if has_prev_block:
        @pl.when(r > 0)
        def _():
            put(kx, 0, kp_ref[...])
            put(vx, 0, vp_ref[...])

    put(kx, window, kc_ref[...])
    put(vx, window, vc_ref[...])

    scale = 1.0 / math.sqrt(head_dim)
    n_chunks = rows // CHUNK
    key_pos = lax.broadcasted_iota(jnp.int32, (SWA_GROUP * CHUNK, n_k), 1)
    nt = lambda a, b: lax.dot_general(a, b, (((1,), (1,)), ((), ())), preferred_element_type=F32)

    def chunk(cc, _):
        q0 = pl.multiple_of(cc * CHUNK, CHUNK)
        qrows = pl.ds(q0, CHUNK)
        krows = pl.ds(q0, n_k)
        if not has_past:
            valid = key_pos + (r * n_chunks + cc) * CHUNK >= window
        logits = []
        for kv in range(n_kv):
            k0, k1 = kx[0, kv, krows, :], kx[1, kv, krows, :]
            parts = []
            for half in range(SWA_GROUP // 2):
                c0 = (kv * SWA_GROUP + 2 * half) * head_dim
                qp = (q_ref[qrows, c0:c0 + pair] * scale).astype(BF16)
                parts += [nt(qp, k0), nt(qp, k1)]
            lg = jnp.concatenate(parts, axis=0) + bias_ref[kv]
            if not has_past:
                lg = jnp.where(valid, lg, -1e30)
            logits.append(lg)
        probs = []
        for kv in range(n_kv):
            sink = sink_ref[kv]
            m = jnp.maximum(jnp.max(logits[kv], axis=-1, keepdims=True), sink)
            p = jnp.exp(logits[kv] - m)
            den = jnp.sum(p, axis=-1, keepdims=True) + jnp.exp(sink - m)
            probs.append((p * (1.0 / den)).astype(BF16))
        for kv in range(n_kv):
            v0, v1 = vx[0, kv, krows, :], vx[1, kv, krows, :]
            for half in range(SWA_GROUP // 2):
                g0 = 2 * half * CHUNK
                out = (jnp.dot(probs[kv][g0:g0 + CHUNK], v0, preferred_element_type=F32)
                       + jnp.dot(probs[kv][g0 + CHUNK:g0 + 2 * CHUNK], v1, preferred_element_type=F32))
                c0 = (kv * SWA_GROUP + 2 * half) * head_dim
                o_ref[qrows, c0:c0 + pair] = out.astype(o_ref.dtype)
        return 0

    lax.fori_loop(0, n_chunks, chunk, 0, unroll=2)


def _swa(h, k_cache, v_cache, sinks, bias, *, n_seq, seq_len, q_col, k_col, v_col, q_width, kv_width, window):
    has_past = k_cache is not None
    rows = min(seq_len, 256)
    has_prev_block = seq_len > rows
    assert rows % CHUNK == 0 and (not has_prev_block or rows % window == 0)
    n_heads = sinks.shape[0]
    head_dim = q_width // n_heads
    n_kv = n_heads // SWA_GROUP
    n_k = window + CHUNK
    blocks_per_seq = seq_len // rows
    qb, kb, vb = q_col // q_width, k_col // kv_width, v_col // kv_width
    assert q_col % q_width == 0 and k_col % kv_width == 0 and v_col % kv_width == 0
    bias_g = bias.reshape(n_kv, SWA_GROUP * CHUNK, n_k)
    sink_g = jnp.repeat(sinks.astype(F32).reshape(n_kv, SWA_GROUP), CHUNK, axis=1)[..., None]

    in_specs = [
        pl.BlockSpec((rows, q_width), lambda b, r: (b * blocks_per_seq + r, qb)),
        pl.BlockSpec((rows, kv_width), lambda b, r: (b * blocks_per_seq + r, kb)),
        pl.BlockSpec((rows, kv_width), lambda b, r: (b * blocks_per_seq + r, vb)),
    ]
    args = [h, h, h]
    if has_prev_block:
        wpb = rows // window
        prev = lambda col: pl.BlockSpec(
            (window, kv_width), lambda b, r: (jnp.maximum((b * blocks_per_seq + r) * wpb - 1, 0), col))
        in_specs += [prev(kb), prev(vb)]
        args += [h, h]
    if has_past:
        cache = pl.BlockSpec((None, window, kv_width), lambda b, r: (b, 0, 0))
        in_specs += [cache, cache]
        args += [k_cache.reshape(n_seq, window, kv_width), v_cache.reshape(n_seq, window, kv_width)]
    in_specs += [_resident(bias_g.shape), _resident(sink_g.shape)]
    args += [bias_g, sink_g]

    return pl.pallas_call(
        functools.partial(_swa_body, rows=rows, window=window, n_kv=n_kv, head_dim=head_dim,
                          has_past=has_past, has_prev_block=has_prev_block),
        grid=(n_seq, blocks_per_seq),
        in_specs=in_specs,
        out_specs=pl.BlockSpec((rows, q_width), lambda b, r: (b * blocks_per_seq + r, 0)),
        out_shape=jax.ShapeDtypeStruct((n_seq * seq_len, q_width), BF16),
        scratch_shapes=[pltpu.VMEM((2, n_kv, window + rows, 2 * head_dim), BF16)] * 2,
        compiler_params=_params(("parallel", "arbitrary")),
        name="swa_attention",
    )(*args)


def _merge(x, y_ssm, attn, h, w_glu, w_o, w_out, *, gs_col, ga_col, tm):
    m, d = x.shape
    gcol = math.gcd(gs_col, ga_col, d)
    assert gcol % LANE == 0
    nsub = d // gcol

    def body(x_ref, y_ref, a_ref, *rest):
        gs = jnp.concatenate([r[...] for r in rest[:nsub]], axis=1)
        ga = jnp.concatenate([r[...] for r in rest[nsub:2 * nsub]], axis=1)
        wglu_ref, wo_ref, wout_ref, o_ref = rest[2 * nsub:]
        gl = jnp.dot(y_ref[...], wglu_ref[...], preferred_element_type=F32)
        ys = gl[:, :d] * jax.nn.sigmoid(gl[:, d:])
        ya = jnp.dot(a_ref[...], wo_ref[...], preferred_element_type=F32)
        merged = jax.nn.sigmoid(gs) * ys + jax.nn.sigmoid(ga) * ya
        o_ref[...] = x_ref[...] + jnp.dot(merged.astype(BF16), wout_ref[...], preferred_element_type=F32)

    gate_specs = [pl.BlockSpec((tm, gcol), lambda i, c=c: (i, c))
                  for base in (gs_col, ga_col) for c in range(base // gcol, base // gcol + nsub)]
    return pl.pallas_call(
        body,
        grid=(m // tm,),
        in_specs=[
            pl.BlockSpec((tm, d), lambda i: (i, 0)),
            pl.BlockSpec((tm, y_ssm.shape[1]), lambda i: (i, 0)),
            pl.BlockSpec((tm, attn.shape[1]), lambda i: (i, 0)),
            *gate_specs,
            _resident(w_glu.shape),
            _resident(w_o.shape),
            _resident(w_out.shape),
        ],
        out_specs=pl.BlockSpec((tm, d), lambda i: (i, 0)),
        out_shape=jax.ShapeDtypeStruct((m, d), F32),
        compiler_params=_params(("parallel",)),
        name="merge_out_proj",
    )(x, y_ssm, attn, *([h] * (2 * nsub)), w_glu, w_o, w_out)


def _xattn_body(x_ref, g_ref, mk_ref, mv_ref, wq_ref, wo_ref, o_ref, *, n_heads, head_dim):
    x = x_ref[...]
    q = jnp.dot(_rms(x, g_ref[...]).astype(BF16), wq_ref[...], preferred_element_type=F32).astype(BF16)
    scale = 1.0 / math.sqrt(head_dim)
    outs = []
    for hd in range(n_heads):
        hs = slice(hd * head_dim, (hd + 1) * head_dim)
        logits = lax.dot_general(q[:, hs], mk_ref[:, hs].astype(BF16), (((1,), (1,)), ((), ())),
                                 preferred_element_type=F32) * scale
        m = jnp.max(logits, axis=-1, keepdims=True)
        p = jnp.exp(logits - m)
        probs = p / jnp.sum(p, axis=-1, keepdims=True)
        outs.append(jnp.dot(probs.astype(BF16), mv_ref[:, hs].astype(BF16), preferred_element_type=F32))
    att = jnp.concatenate(outs, axis=1).astype(BF16)
    o_ref[...] = x + jnp.dot(att, wo_ref[...], preferred_element_type=F32)


def _xattn(x, g, mk, mv, w_q, w_o, *, n_seq, seq_len, n_heads):
    m, d = x.shape
    tm = min(seq_len, 256)
    blocks_per_seq = seq_len // tm
    mem_len, xw = mk.shape[1], mk.shape[2]
    return pl.pallas_call(
        functools.partial(_xattn_body, n_heads=n_heads, head_dim=xw // n_heads),
        grid=(n_seq, blocks_per_seq),
        in_specs=[
            pl.BlockSpec((tm, d), lambda b, r: (b * blocks_per_seq + r, 0)),
            _resident((1, d)),
            pl.BlockSpec((None, mem_len, xw), lambda b, r: (b, 0, 0)),
            pl.BlockSpec((None, mem_len, xw), lambda b, r: (b, 0, 0)),
            _resident(w_q.shape),
            _resident(w_o.shape),
        ],
        out_specs=pl.BlockSpec((tm, d), lambda b, r: (b * blocks_per_seq + r, 0)),
        out_shape=jax.ShapeDtypeStruct((m, d), F32),
        compiler_params=_params(("parallel", "arbitrary")),
        name="cross_attention",
    )(x, g.reshape(1, d), mk, mv, w_q, w_o)


def _router_body(xp_ref, xs_ref, g_ref, wr_ref, br_ref, xn_ref, idx_ref, gate_ref, rank_ref, cnt_ref, carry,
                 *, n_exp, n_prompt_tiles):
    i = pl.program_id(0)

    @pl.when(i == 0)
    def _():
        carry[...] = jnp.zeros_like(carry)

    xn = _rms(jnp.where(i < n_prompt_tiles, xp_ref[...], xs_ref[...]), g_ref[...])
    xn_ref[...] = xn
    tm = xn.shape[0]
    logits = lax.dot_general(wr_ref[...], xn, (((1,), (1,)), ((), ())),
                             precision=lax.Precision.HIGHEST, preferred_element_type=F32) + br_ref[...]
    eid = lax.broadcasted_iota(jnp.int32, (n_exp, tm), 0)
    vals, hots = [], []
    work = logits
    for k in range(TOP_K):
        m = jnp.max(work, axis=0, keepdims=True)
        sel = jnp.min(jnp.where(work == m, eid, n_exp), axis=0, keepdims=True)
        hot = eid == sel
        vals.append(m)
        hots.append(hot)
        idx_ref[k:k + 1, :] = sel
        work = jnp.where(hot, -jnp.inf, work)
    ex = [jnp.exp(v - vals[0]) for v in vals]
    tot = ex[0] + ex[1] + ex[2] + ex[3]
    for k in range(TOP_K):
        gate_ref[k:k + 1, :] = ex[k] / tot

    chosen = (hots[0] | hots[1] | hots[2] | hots[3])
    a = jnp.where(chosen, 1.0, 0.0)
    earlier = (lax.broadcasted_iota(jnp.int32, (tm, tm), 0) < lax.broadcasted_iota(jnp.int32, (tm, tm), 1))
    prefix = jnp.dot(a.astype(BF16), jnp.where(earlier, 1.0, 0.0).astype(BF16),
                     preferred_element_type=F32) + carry[...]
    for k in range(TOP_K):
        rank_ref[k:k + 1, :] = jnp.sum(jnp.where(hots[k], prefix, 0.0), axis=0, keepdims=True).astype(jnp.int32)
    carry[...] = carry[...] + jnp.sum(a, axis=1, keepdims=True)
    cnt_ref[...] = carry[...].astype(jnp.int32)


def _two_group_specs(tm, d, npt):
    return [pl.BlockSpec((tm, d), lambda i: (jnp.minimum(i, npt - 1), 0)),
            pl.BlockSpec((tm, d), lambda i: (jnp.maximum(i - npt, 0), 0))]


def _router(x_p, x_s, g, w_router, b_router):
    d = x_p.shape[1]
    t = x_p.shape[0] + x_s.shape[0]
    n_exp = w_router.shape[1]
    tm = ROUTER_TM
    assert x_p.shape[0] % tm == 0 and x_s.shape[0] % tm == 0
    npt = x_p.shape[0] // tm
    row = pl.BlockSpec((TOP_K, tm), lambda i: (0, i))
    return pl.pallas_call(
        functools.partial(_router_body, n_exp=n_exp, n_prompt_tiles=npt),
        grid=(t // tm,),
        in_specs=[
            *_two_group_specs(tm, d, npt),
            _resident((1, d)),
            _resident((n_exp, d)),
            _resident((n_exp, 1)),
        ],
        out_specs=[pl.BlockSpec((tm, d), lambda i: (i, 0)), row, row, row,
                   pl.BlockSpec((n_exp, 1), lambda i: (0, 0))],
        out_shape=[
            jax.ShapeDtypeStruct((t, d), F32),
            jax.ShapeDtypeStruct((TOP_K, t), jnp.int32),
            jax.ShapeDtypeStruct((TOP_K, t), F32),
            jax.ShapeDtypeStruct((TOP_K, t), jnp.int32),
            jax.ShapeDtypeStruct((n_exp, 1), jnp.int32),
        ],
        scratch_shapes=[pltpu.VMEM((n_exp, 1), F32)],
        compiler_params=_params(("arbitrary",)),
        name="moe_router",
    )(x_p, x_s, g.reshape(1, d), w_router.astype(F32).T, b_router.astype(F32).reshape(n_exp, 1))


def _dispatch_body(pos_ref, fill_ref, x_ref, xs_ref, zeros, sem, fill_sem, *, n_exp):
    tm, d = x_ref.shape
    fill_rows = zeros.shape[0]

    def fill_copy(e):
        start = pl.multiple_of(fill_ref[e], SUBLANE)
        return pltpu.make_async_copy(zeros, xs_ref.at[pl.ds(start, fill_rows), :], fill_sem)

    @pl.when(pl.program_id(0) == 0)
    def _():
        zeros[...] = jnp.zeros_like(zeros)
        for e in range(n_exp):
            fill_copy(e).start()
        for e in range(n_exp):
            fill_copy(e).wait()

    def issue(t, _):
        for k in range(TOP_K):
            pltpu.make_async_copy(x_ref.at[pl.ds(t, 1), :], xs_ref.at[pl.ds(pos_ref[k, t], 1), :], sem).start()
        return 0

    lax.fori_loop(0, tm, issue, 0, unroll=8)
    for k in range(TOP_K):
        pltpu.make_async_copy(x_ref, xs_ref.at[pl.ds(0, tm), :], sem).wait()


def _dispatch(xn, pos, fill_start, n_rows):
    t, d = xn.shape
    tm = TOK_TM
    n_exp = fill_start.shape[0]
    fill_rows = MOE_SB + SUBLANE
    return pl.pallas_call(
        functools.partial(_dispatch_body, n_exp=n_exp),
        grid=(t // tm,),
        in_specs=[
            pl.BlockSpec((TOP_K, tm), lambda i: (0, i), memory_space=pltpu.SMEM),
            pl.BlockSpec(memory_space=pltpu.SMEM),
            pl.BlockSpec((tm, d), lambda i: (i, 0)),
        ],
        out_specs=pl.BlockSpec(memory_space=pl.ANY),
        out_shape=jax.ShapeDtypeStruct((n_rows + fill_rows, d), xn.dtype),
        scratch_shapes=[pltpu.VMEM((fill_rows, d), xn.dtype), pltpu.SemaphoreType.DMA, pltpu.SemaphoreType.DMA],
        compiler_params=_params(("arbitrary",)),
        name="moe_dispatch",
    )(pos, fill_start, xn)


def _experts_body(te_ref, row0_ref, nv_ref, xs_ref, wg_ref, wl_ref, bg_ref, bl_ref, wd_ref, bd_ref,
                  ys_ref, acc, xb, stage, wgb, wlb, wdb, in_sem, out_sem, *, n_j, balanced_sub):
    del te_ref
    i, j = pl.program_id(0), pl.program_id(1)
    n_valid = nv_ref[i]
    row0 = row0_ref[i]
    d = acc.shape[1]
    n_sub = (n_valid + MOE_SB - 1) // MOE_SB
    n_big = n_valid // MOE_BIG
    small0 = n_big * MOE_BIG
    n_small = n_sub - n_big * (MOE_BIG // MOE_SB)

    def in_copy(s, slot):
        src = pl.ds(pl.multiple_of(row0 + s * MOE_SB, MOE_SB), MOE_SB)
        return pltpu.make_async_copy(xs_ref.at[src, :], stage.at[slot], in_sem.at[slot])

    def out_copy(first, size):
        src = pl.ds(pl.multiple_of(first, MOE_SB), size)
        dst = pl.ds(pl.multiple_of(row0 + first, MOE_SB), size)
        return pltpu.make_async_copy(acc.at[src, :], ys_ref.at[dst, :], out_sem)

    def compute(first, size):
        rows = pl.ds(pl.multiple_of(first, MOE_SB), size)
        xs = xb[rows, :]
        hg = jnp.dot(xs, wgb[...], preferred_element_type=F32) + bg_ref[...]
        hl = jnp.dot(xs, wlb[...], preferred_element_type=F32) + bl_ref[...]
        glu = jnp.minimum(hg, SWIGLU_LIMIT)
        lin = jnp.clip(hl, -SWIGLU_LIMIT, SWIGLU_LIMIT)
        act = (glu * jax.nn.sigmoid(SWIGLU_ALPHA * glu) * (lin + 1.0)).astype(BF16)
        for c in range(0, d, MOE_DOWN_COLS):
            cols = slice(c, c + MOE_DOWN_COLS)
            acc[rows, cols] += jnp.dot(act, wdb[:, cols], preferred_element_type=F32)

        @pl.when(j == n_j - 1)
        def _():
            out_copy(first, size).start()

    @pl.when(n_valid > 0)
    def _():
        @pl.when(j == 0)
        def _():
            in_copy(0, 0).start()

            def load(s, _):
                slot = s % 2
                rows = pl.ds(pl.multiple_of(s * MOE_SB, MOE_SB), MOE_SB)

                @pl.when(s + 1 < n_sub)
                def _():
                    in_copy(s + 1, 1 - slot).start()

                in_copy(s, slot).wait()
                xb[rows, :] = stage[slot].astype(BF16)
                acc[rows, :] = jnp.broadcast_to(bd_ref[...], (MOE_SB, d))
                return 0

            lax.fori_loop(0, n_sub, load, 0)

        wgb[...] = wg_ref[...].astype(BF16)
        wlb[...] = wl_ref[...].astype(BF16)
        wdb[...] = wd_ref[...].astype(BF16)

        def big(s, _):
            compute(s * MOE_BIG, MOE_BIG)
            return 0

        def small(s, _):
            compute(small0 + s * MOE_SB, MOE_SB)
            return 0

        def drain_big(s, _):
            out_copy(s * MOE_BIG, MOE_BIG).wait()
            return 0

        def drain_small(s, _):
            out_copy(small0 + s * MOE_SB, MOE_SB).wait()
            return 0

        balanced = n_sub == balanced_sub

        @pl.when(balanced)
        def _():
            compute(0, balanced_sub * MOE_SB)

        @pl.when(jnp.logical_not(balanced))
        def _():
            lax.fori_loop(0, n_big, big, 0)
            lax.fori_loop(0, n_small, small, 0)

        @pl.when((j == n_j - 1) & balanced)
        def _():
            out_copy(0, balanced_sub * MOE_SB).wait()

        @pl.when((j == n_j - 1) & jnp.logical_not(balanced))
        def _():
            lax.fori_loop(0, n_big, drain_big, 0)
            lax.fori_loop(0, n_small, drain_small, 0)


def _experts(xs, tile_expert, tile_row0, tile_valid, n_rows, n_assign, w_up, b_up, w_down, b_down):
    d = xs.shape[1]
    n_tiles = tile_expert.shape[0]
    depth, n_exp, _, ff2 = w_up.shape
    assert depth == 1
    balanced_sub = min(-(-n_assign // (n_exp * MOE_SB)), MOE_CAP // MOE_SB)
    d_ff = ff2 // 2
    n_j = d_ff // MOE_TF
    b_up = b_up.reshape(depth, n_exp, 1, ff2)
    b_down = b_down.reshape(depth, n_exp, 1, d)

    def jsel(i, j, nv):
        return jnp.where(nv[i] > 0, j, n_j - 1)

    grid_spec = pltpu.PrefetchScalarGridSpec(
        num_scalar_prefetch=3,
        grid=(n_tiles, n_j),
        in_specs=[
            pl.BlockSpec(memory_space=pl.ANY),
            pl.BlockSpec((None, None, d, MOE_TF), lambda i, j, te, r0, nv: (0, te[i], 0, jsel(i, j, nv))),
            pl.BlockSpec((None, None, d, MOE_TF), lambda i, j, te, r0, nv: (0, te[i], 0, n_j + jsel(i, j, nv))),
            pl.BlockSpec((None, None, 1, MOE_TF), lambda i, j, te, r0, nv: (0, te[i], 0, jsel(i, j, nv))),
            pl.BlockSpec((None, None, 1, MOE_TF), lambda i, j, te, r0, nv: (0, te[i], 0, n_j + jsel(i, j, nv))),
            pl.BlockSpec((None, None, MOE_TF, d), lambda i, j, te, r0, nv: (0, te[i], jsel(i, j, nv), 0)),
            pl.BlockSpec((None, None, 1, d), lambda i, j, te, r0, nv: (0, te[i], 0, 0)),
        ],
        out_specs=pl.BlockSpec(memory_space=pl.ANY),
        scratch_shapes=[
            pltpu.VMEM((MOE_CAP, d), F32),
            pltpu.VMEM((MOE_CAP, d), BF16),
            pltpu.VMEM((2, MOE_SB, d), F32),
            pltpu.VMEM((d, MOE_TF), BF16),
            pltpu.VMEM((d, MOE_TF), BF16),
            pltpu.VMEM((MOE_TF, d), BF16),
            pltpu.SemaphoreType.DMA((2,)),
            pltpu.SemaphoreType.DMA,
        ],
    )
    return pl.pallas_call(
        functools.partial(_experts_body, n_j=n_j, balanced_sub=balanced_sub),
        grid_spec=grid_spec,
        out_shape=jax.ShapeDtypeStruct((n_rows, d), F32),
        compiler_params=_params(("arbitrary", "arbitrary")),
        name="moe_experts",
    )(tile_expert, tile_row0, tile_valid, xs, w_up, w_up, b_up, b_up, w_down, b_down)


def _combine_body(pos_ref, nxt_ref, xp_ref, xs_ref, gate_ref, g_ref, ys_ref, op_ref, os_ref, buf, sem,
                  *, n_prompt_tiles):
    i = pl.program_id(0)
    n = pl.num_programs(0)
    tm = xp_ref.shape[0]

    def gather(p_ref, slot):
        def issue(t, _):
            for k in range(TOP_K):
                pltpu.make_async_copy(ys_ref.at[pl.ds(p_ref[k, t], 1), :], buf.at[slot, k, pl.ds(t, 1), :],
                                      sem.at[slot]).start()
            return 0

        lax.fori_loop(0, tm, issue, 0, unroll=8)

    slot = i % 2

    @pl.when(i == 0)
    def _():
        gather(pos_ref, 0)

    @pl.when(i + 1 < n)
    def _():
        gather(nxt_ref, 1 - slot)

    for k in range(TOP_K):
        pltpu.make_async_copy(ys_ref.at[pl.ds(0, tm), :], buf.at[slot, k], sem.at[slot]).wait()

    out = jnp.where(i < n_prompt_tiles, xp_ref[...], xs_ref[...])
    for k in range(TOP_K):
        out = out + gate_ref[:, k:k + 1] * buf[slot, k]
    y = _rms(out, g_ref[...])

    @pl.when(i < n_prompt_tiles)
    def _():
        op_ref[...] = y

    @pl.when(i >= n_prompt_tiles)
    def _():
        os_ref[...] = y


def _combine(x_p, x_s, ys, pos, gate_t, g_final):
    n_prompt, d = x_p.shape
    t = n_prompt + x_s.shape[0]
    tm = TOK_TM
    assert n_prompt % tm == 0 and t % tm == 0
    npt = n_prompt // tm
    n = t // tm
    return pl.pallas_call(
        functools.partial(_combine_body, n_prompt_tiles=npt),
        grid=(n,),
        in_specs=[
            pl.BlockSpec((TOP_K, tm), lambda i: (0, i), memory_space=pltpu.SMEM),
            pl.BlockSpec((TOP_K, tm), lambda i: (0, jnp.minimum(i + 1, n - 1)), memory_space=pltpu.SMEM),
            *_two_group_specs(tm, d, npt),
            pl.BlockSpec((tm, TOP_K), lambda i: (i, 0)),
            _resident((1, d)),
            pl.BlockSpec(memory_space=pl.ANY),
        ],
        out_specs=_two_group_specs(tm, d, npt),
        out_shape=[
            jax.ShapeDtypeStruct((n_prompt, d), F32),
            jax.ShapeDtypeStruct((t - n_prompt, d), F32),
        ],
        scratch_shapes=[pltpu.VMEM((2, TOP_K, tm, d), F32), pltpu.SemaphoreType.DMA((2,))],
        compiler_params=_params(("arbitrary",)),
        name="moe_combine",
    )(pos, pos, x_p, x_s, gate_t, g_final.reshape(1, d), ys)


def _moe_layout(idx, rank, counts, n_tiles):
    n_exp = counts.shape[0]
    region = ((counts + MOE_SB - 1) // MOE_SB) * MOE_SB
    row_start = jnp.cumsum(region) - region
    tiles_per_exp = (counts + MOE_CAP - 1) // MOE_CAP
    tile_end = jnp.cumsum(tiles_per_exp)
    tile_start = tile_end - tiles_per_exp
    n_used = tile_end[-1]
    tiles = jnp.arange(n_tiles, dtype=jnp.int32)
    te = jnp.minimum(jnp.searchsorted(tile_end, tiles, side="right"), n_exp - 1).astype(jnp.int32)
    te = jnp.where(tiles < n_used, te, te[jnp.maximum(n_used - 1, 0)])
    first = (tiles - tile_start[te]) * MOE_CAP
    valid = jnp.where(tiles < n_used, jnp.clip(counts[te] - first, 0, MOE_CAP), 0)
    row0 = jnp.where(tiles < n_used, row_start[te] + first, 0)
    hot = idx[..., None] == jnp.arange(n_exp, dtype=jnp.int32)
    pos = rank + jnp.sum(jnp.where(hot, row_start, 0), axis=-1)
    fill_start = row_start + (counts // SUBLANE) * SUBLANE
    i32 = lambda a: a.astype(jnp.int32)
    return i32(pos), te, i32(row0), i32(valid), i32(fill_start)


def kernel(x_prompt, x_sample, state_ssm_re, state_ssm_im, cache_swa_k, cache_swa_v, cache_mem_k, cache_mem_v, mem_prompt, w_in, ssm_lam_re, ssm_lam_im, ssm_log_dt, ssm_b_re, ssm_b_im, ssm_c_re, ssm_c_im, ssm_d, w_glu, swa_sinks, rel_bias, w_o_attn, w_out, w_xq, w_mk, w_mv, w_xo, w_router, b_router, w_up, b_up, w_down, b_down, norm_mix, norm_x, norm_mem, norm_moe, norm_final):
    depth = w_in.shape[0]
    assert depth == 1, "single-layer step"
    bp, lp, d = x_prompt.shape
    bs, ls, _ = x_sample.shape
    n_p, n_s = bp * lp, bs * ls
    t = n_p + n_s
    _, _, window, kvh, hd = cache_swa_k.shape
    kv_width = kvh * hd
    n_heads = swa_sinks.shape[1]
    q_width = n_heads * hd
    ssm_width = ssm_d.shape[1]
    _, _, mem_len, x_heads, x_hd = cache_mem_k.shape
    xw = x_heads * x_hd
    ssm_groups, ssm_state = ssm_lam_re.shape[1], ssm_lam_re.shape[2]
    q_col = ssm_width
    k_col = q_col + q_width
    v_col = k_col + kv_width
    gs_col = v_col + kv_width
    ga_col = gs_col + d

    l = 0
    w_in_b = w_in[l].astype(BF16)
    w_glu_b, w_o_b, w_out_b = w_glu[l].astype(BF16), w_o_attn[l].astype(BF16), w_out[l].astype(BF16)
    w_xq_b, w_xo_b = w_xq[l].astype(BF16), w_xo[l].astype(BF16)
    w_mkv_b = jnp.concatenate([w_mk[l], w_mv[l]], axis=1).astype(BF16)
    a_re, a_im, bcat, ccat = _s5_discretize(ssm_lam_re[l], ssm_lam_im[l], ssm_log_dt[l], ssm_b_re[l],
                                            ssm_b_im[l], ssm_c_re[l], ssm_c_im[l])
    bucket = _t5_bucket_table(CHUNK, window + CHUNK, window)
    bias = jnp.transpose(rel_bias.astype(F32)[bucket], (2, 0, 1))

    def mixers(x, n_seq, seq_len, s0_re, s0_im, k_cache, v_cache):
        xf = x.reshape(n_seq * seq_len, d)
        h = _norm_matmul(xf, norm_mix[l], w_in_b, tm=1024, tn=1664)
        y_ssm, s_re, s_im = _s5(h[:, :ssm_width], s0_re, s0_im, a_re, a_im, bcat, ccat, ssm_d[l],
                                n_seq=n_seq, seq_len=seq_len)
        attn = _swa(h, k_cache, v_cache, swa_sinks[l], bias, n_seq=n_seq, seq_len=seq_len, q_col=q_col,
                    k_col=k_col, v_col=v_col, q_width=q_width, kv_width=kv_width, window=window)
        x1 = _merge(xf, y_ssm, attn, h, w_glu_b, w_o_b, w_out_b, gs_col=gs_col, ga_col=ga_col, tm=256)
        h3 = h.reshape(n_seq, seq_len, -1)
        k_new = h3[:, :, k_col:k_col + kv_width]
        v_new = h3[:, :, v_col:v_col + kv_width]
        if k_cache is not None:
            k_new = jnp.concatenate([k_cache.reshape(n_seq, window, kv_width), k_new], axis=1)
            v_new = jnp.concatenate([v_cache.reshape(n_seq, window, kv_width), v_new], axis=1)
        k_new = k_new[:, -window:].reshape(1, n_seq, window, kvh, hd)
        v_new = v_new[:, -window:].reshape(1, n_seq, window, kvh, hd)
        state_shape = (1, n_seq, ssm_groups, ssm_state)
        return x1, s_re.reshape(state_shape), s_im.reshape(state_shape), k_new, v_new

    zero_state = jnp.zeros((bp, ssm_groups * ssm_state), F32)
    x1_p, sre_p, sim_p, k_p, v_p = mixers(x_prompt, bp, lp, zero_state, zero_state, None, None)
    x1_s, sre_s, sim_s, k_s, v_s = mixers(x_sample, bs, ls, state_ssm_re[l], state_ssm_im[l],
                                          cache_swa_k[l], cache_swa_v[l])

    mkv = _norm_matmul(mem_prompt.reshape(bp * mem_len, d), norm_mem[l], w_mkv_b, tm=1024, tn=512)
    mkv = mkv.reshape(bp, mem_len, 2 * xw)
    mk_p, mv_p = mkv[:, :, :xw], mkv[:, :, xw:]

    x2_p = _xattn(x1_p, norm_x[l], mk_p, mv_p, w_xq_b, w_xo_b, n_seq=bp, seq_len=lp, n_heads=x_heads)
    x2_s = _xattn(x1_s, norm_x[l], cache_mem_k[l].reshape(bs, mem_len, xw), cache_mem_v[l].reshape(bs, mem_len, xw),
                  w_xq_b, w_xo_b, n_seq=bs, seq_len=ls, n_heads=x_heads)

    n_exp = w_router.shape[2]
    xn, idx, gate, rank, counts = _router(x2_p, x2_s, norm_moe[l], w_router[l], b_router[l])
    n_tiles = (t * TOP_K) // MOE_CAP + n_exp
    n_rows = t * TOP_K + n_exp * MOE_SB
    pos, tile_expert, tile_row0, tile_valid, fill_start = _moe_layout(idx, rank, counts[:, 0], n_tiles)
    xs = _dispatch(xn, pos, fill_start, n_rows)
    ys = _experts(xs, tile_expert, tile_row0, tile_valid, n_rows, t * TOP_K, w_up, b_up, w_down, b_down)
    y_p, y_s = _combine(x2_p, x2_s, ys, pos, gate.T, norm_final)

    return (y_p.reshape(bp, lp, d), y_s.reshape(bs, ls, d), sre_p, sim_p, k_p, v_p,
            mk_p.reshape(1, bp, mem_len, x_heads, x_hd), mv_p.reshape(1, bp, mem_len, x_heads, x_hd),
            sre_s, sim_s, k_s, v_s)
```

```python
import functools
import math

import numpy as np
import jax
import jax.numpy as jnp
from jax import lax
from jax.experimental import pallas as pl
from jax.experimental.pallas import tpu as pltpu

F32 = jnp.float32
BF16 = jnp.bfloat16

LANE = 128
SUBLANE = 8
VMEM_LIMIT = 56 * 1024 * 1024

CHUNK = 64
SSM_GROUP = 16
SWA_GROUP = 4
REL_BUCKETS = 32
REL_MAX_DIST = 128
TOP_K = 4
SWIGLU_LIMIT = 7.0
SWIGLU_ALPHA = 1.702
RMS_EPS = 1e-5
SEQ_PER_SCAN = SUBLANE

S5_STRIP = 512
MOE_CAP = 2560
MOE_SB = 256
MOE_BIG = 1024
MOE_DOWN_COLS = 512
MOE_TF = 256
TOK_TM = 256
ROUTER_TM = 512


def _params(sem, vmem=VMEM_LIMIT):
    return pltpu.CompilerParams(dimension_semantics=sem, vmem_limit_bytes=vmem)


def _rms(x, g):
    return x * lax.rsqrt(jnp.mean(x * x, axis=-1, keepdims=True) + RMS_EPS) * g


def _resident(shape):
    nd = len(shape)
    return pl.BlockSpec(shape, lambda *_: (0,) * nd, pipeline_mode=pl.Buffered(1))


def _norm_matmul_body(x_ref, g_ref, w_ref, o_ref, xn_ref):
    @pl.when(pl.program_id(1) == 0)
    def _():
        xn_ref[...] = _rms(x_ref[...], g_ref[...]).astype(BF16)

    o_ref[...] = jnp.dot(xn_ref[...], w_ref[...], preferred_element_type=F32)


def _norm_matmul(x, g, w, *, tm, tn):
    m, d = x.shape
    n = w.shape[1]
    tm = min(tm, m)
    assert m % tm == 0 and n % tn == 0
    return pl.pallas_call(
        _norm_matmul_body,
        grid=(m // tm, n // tn),
        in_specs=[
            pl.BlockSpec((tm, d), lambda i, j: (i, 0)),
            pl.BlockSpec((1, d), lambda i, j: (0, 0)),
            pl.BlockSpec((d, tn), lambda i, j: (0, j)),
        ],
        out_specs=pl.BlockSpec((tm, tn), lambda i, j: (i, j)),
        out_shape=jax.ShapeDtypeStruct((m, n), F32),
        scratch_shapes=[pltpu.VMEM((tm, d), BF16)],
        compiler_params=_params(("parallel", "arbitrary")),
        name="norm_matmul",
    )(x, g.reshape(1, d), w)


def _s5_discretize(lam_re, lam_im, log_dt, b_re, b_im, c_re, c_im):
    g, p = lam_re.shape
    gpb = LANE // SSM_GROUP
    nblk = g // gpb
    lr, li = lam_re.astype(F32), lam_im.astype(F32)
    dt = jnp.exp(log_dt.astype(F32))[:, None]
    mag = jnp.exp(lr * dt)
    a_re = mag * jnp.cos(li * dt)
    a_im = mag * jnp.sin(li * dt)
    den = lr * lr + li * li
    f_re = ((a_re - 1.0) * lr + a_im * li) / den
    f_im = (a_im * lr - (a_re - 1.0) * li) / den
    br, bi = b_re.astype(F32), b_im.astype(F32)
    bb_re = f_re[..., None] * br - f_im[..., None] * bi
    bb_im = f_re[..., None] * bi + f_im[..., None] * br
    eye = jnp.eye(gpb, dtype=F32)

    def b_blocks(bb):
        t = jnp.transpose(bb, (0, 2, 1)).reshape(nblk, gpb, SSM_GROUP, p)
        return jnp.einsum("jghp,gk->jghkp", t, eye).reshape(nblk, gpb * SSM_GROUP, gpb * p)

    def c_blocks(c):
        t = c.astype(F32).reshape(nblk, gpb, SSM_GROUP, p)
        return jnp.einsum("jghp,gk->jgpkh", t, eye).reshape(nblk, gpb * p, gpb * SSM_GROUP)

    bcat = jnp.concatenate([b_blocks(bb_re), b_blocks(bb_im)], axis=2).astype(BF16)
    ccat = jnp.concatenate([c_blocks(c_re), -c_blocks(c_im)], axis=1).astype(BF16)
    bro = lambda a: jnp.broadcast_to(a.reshape(1, g * p), (SUBLANE, g * p))
    return bro(a_re), bro(a_im), bcat, ccat


def _s5_body(u_ref, s0re_ref, s0im_ref, are_ref, aim_ref, bcat_ref, ccat_ref, d_ref,
             y_ref, ore_ref, oim_ref, sre, sim, st_re, st_im, *, tc, nblk, blk_state):
    c = pl.program_id(1)

    @pl.when(c == 0)
    def _():
        st_re[...] = s0re_ref[...]
        st_im[...] = s0im_ref[...]

    for j in range(nblk):
        ub = u_ref[:, j * LANE:(j + 1) * LANE].astype(BF16)
        bu = jnp.dot(ub, bcat_ref[j], preferred_element_type=F32)
        sre[:, j * blk_state:(j + 1) * blk_state] = bu[:, :blk_state]
        sim[:, j * blk_state:(j + 1) * blk_state] = bu[:, blk_state:]

    n_state = sre.shape[1]
    for strip in range(n_state // S5_STRIP):
        sl = slice(strip * S5_STRIP, (strip + 1) * S5_STRIP)
        ar = are_ref[:, sl]
        ai = aim_ref[:, sl]

        def step(t, carry, sl=sl, ar=ar, ai=ai):
            sr, si = carry
            rows = pl.ds(pl.multiple_of(t * SUBLANE, SUBLANE), SUBLANE)
            nr = ar * sr - ai * si + sre[rows, sl]
            ni = ar * si + ai * sr + sim[rows, sl]
            sre[rows, sl] = nr
            sim[rows, sl] = ni
            return nr, ni

        sr, si = lax.fori_loop(0, tc, step, (st_re[:, sl], st_im[:, sl]), unroll=4)
        st_re[:, sl] = sr
        st_im[:, sl] = si

    for j in range(nblk):
        cols = slice(j * blk_state, (j + 1) * blk_state)
        scat = jnp.concatenate([sre[:, cols].astype(BF16), sim[:, cols].astype(BF16)], axis=1)
        yj = jnp.dot(scat, ccat_ref[j], preferred_element_type=F32)
        lanes = slice(j * LANE, (j + 1) * LANE)
        yj = yj + d_ref[:, lanes] * u_ref[:, lanes]
        y_ref[:, lanes] = jax.nn.gelu(yj).astype(y_ref.dtype)

    @pl.when(c == pl.num_programs(1) - 1)
    def _():
        ore_ref[...] = st_re[...]
        oim_ref[...] = st_im[...]


def _s5(u, s0_re, s0_im, a_re, a_im, bcat, ccat, d_skip, *, n_seq, seq_len):
    w = u.shape[1]
    n_state = a_re.shape[1]
    nblk = bcat.shape[0]
    ng = n_seq // SEQ_PER_SCAN
    tc = min(seq_len, 32)
    rows = tc * SEQ_PER_SCAN
    u_tb = u.reshape(ng, SEQ_PER_SCAN, seq_len, w).transpose(0, 2, 1, 3).reshape(ng, seq_len * SEQ_PER_SCAN, w)
    s0_re = s0_re.astype(F32).reshape(ng, SEQ_PER_SCAN, n_state)
    s0_im = s0_im.astype(F32).reshape(ng, SEQ_PER_SCAN, n_state)
    state_spec = pl.BlockSpec((None, SEQ_PER_SCAN, n_state), lambda g, c: (g, 0, 0))
    y_tb, o_re, o_im = pl.pallas_call(
        functools.partial(_s5_body, tc=tc, nblk=nblk, blk_state=n_state // nblk),
        grid=(ng, seq_len // tc),
        in_specs=[
            pl.BlockSpec((None, rows, w), lambda g, c: (g, c, 0)),
            state_spec,
            state_spec,
            _resident(a_re.shape),
            _resident(a_im.shape),
            _resident(bcat.shape),
            _resident(ccat.shape),
            _resident((1, w)),
        ],
        out_specs=[
            pl.BlockSpec((None, rows, w), lambda g, c: (g, c, 0)),
            state_spec,
            state_spec,
        ],
        out_shape=[
            jax.ShapeDtypeStruct((ng, seq_len * SEQ_PER_SCAN, w), BF16),
            jax.ShapeDtypeStruct((ng, SEQ_PER_SCAN, n_state), F32),
            jax.ShapeDtypeStruct((ng, SEQ_PER_SCAN, n_state), F32),
        ],
        scratch_shapes=[
            pltpu.VMEM((rows, n_state), F32),
            pltpu.VMEM((rows, n_state), F32),
            pltpu.VMEM((SEQ_PER_SCAN, n_state), F32),
            pltpu.VMEM((SEQ_PER_SCAN, n_state), F32),
        ],
        compiler_params=_params(("parallel", "arbitrary")),
        name="s5_scan",
    )(u_tb, s0_re, s0_im, a_re, a_im, bcat, ccat, d_skip.astype(F32).reshape(1, w))
    y = y_tb.reshape(ng, seq_len, SEQ_PER_SCAN, w).transpose(0, 2, 1, 3).reshape(n_seq * seq_len, w)
    return y, o_re.reshape(n_seq, n_state), o_im.reshape(n_seq, n_state)


def _t5_bucket_table(n_q, n_k, n_prev):
    half = REL_BUCKETS // 2
    max_exact = half // 2
    assert (REL_MAX_DIST // max_exact) ** 2 == 2 ** (half - max_exact)
    rel = np.arange(n_k)[None, :] - n_prev - np.arange(n_q)[:, None]
    n = np.abs(rel)
    k = np.zeros_like(n)
    for kk in range(1, 2 * half):
        k = np.where(n * n >= max_exact * max_exact * (2 ** kk), kk, k)
    large = np.minimum(max_exact + k, half - 1)
    return ((rel > 0) * half + np.where(n < max_exact, n, large)).astype(np.int32)


def _swa_body(*refs, rows, window, n_kv, head_dim, has_past, has_prev_block):
    it = iter(refs)
    q_ref, kc_ref, vc_ref = next(it), next(it), next(it)
    kp_ref = vp_ref = kcache_ref = vcache_ref = None
    if has_prev_block:
        kp_ref, vp_ref = next(it), next(it)
    if has_past:
        kcache_ref, vcache_ref = next(it), next(it)
    bias_ref, sink_ref = next(it), next(it)
    o_ref = next(it)
    kx, vx = next(it), next(it)

    r = pl.program_id(1)
    n_k = window + CHUNK
    pair = 2 * head_dim

    def put(dst, lo, src):
        n = src.shape[0]
        zero = jnp.zeros((n, head_dim), F32)
        for kv in range(n_kv):
            x = src[:, kv * head_dim:(kv + 1) * head_dim]
            dst[0, kv, lo:lo + n, :] = jnp.concatenate([x, zero], axis=1).astype(BF16)
            dst[1, kv, lo:lo + n, :] = jnp.concatenate([zero, x], axis=1).astype(BF16)

    @pl.when(r == 0)
    def _():
        if has_past:
            put(kx, 0, kcache_ref[...])
            put(vx, 0, vcache_ref[...])
        else:
            kx[:, :, 0:window, :] = jnp.zeros((2, n_kv, window, pair), BF16)
            vx[:, :, 0:window, :] = jnp.zeros((2, n_kv, window, pair), BF16)

    if has_prev_block:
        @pl.when(r > 0)
        def _():
            put(kx, 0, kp_ref[...])
            put(vx, 0, vp_ref[...])

    put(kx, window, kc_ref[...])
    put(vx, window, vc_ref[...])

    scale = 1.0 / math.sqrt(head_dim)
    n_chunks = rows // CHUNK
    key_pos = lax.broadcasted_iota(jnp.int32, (SWA_GROUP * CHUNK, n_k), 1)
    nt = lambda a, b: lax.dot_general(a, b, (((1,), (1,)), ((), ())), preferred_element_type=F32)

    def chunk(cc, _):
        q0 = pl.multiple_of(cc * CHUNK, CHUNK)
        qrows = pl.ds(q0, CHUNK)
        krows = pl.ds(q0, n_k)
        if not has_past:
            valid = key_pos + (r * n_chunks + cc) * CHUNK >= window
        logits = []
        for kv in range(n_kv):
            k0, k1 = kx[0, kv, krows, :], kx[1, kv, krows, :]
            parts = []
            for half in range(SWA_GROUP // 2):
                c0 = (kv * SWA_GROUP + 2 * half) * head_dim
                qp = (q_ref[qrows, c0:c0 + pair] * scale).astype(BF16)
                parts += [nt(qp, k0), nt(qp, k1)]
            lg = jnp.concatenate(parts, axis=0) + bias_ref[kv]
            if not has_past:
                lg = jnp.where(valid, lg, -1e30)
            logits.append(lg)
        probs = []
        for kv in range(n_kv):
            sink = sink_ref[kv]
            m = jnp.maximum(jnp.max(logits[kv], axis=-1, keepdims=True), sink)
            p = jnp.exp(logits[kv] - m)
            den = jnp.sum(p, axis=-1, keepdims=True) + jnp.exp(sink - m)
            probs.append((p * (1.0 / den)).astype(BF16))
        for kv in range(n_kv):
            v0, v1 = vx[0, kv, krows, :], vx[1, kv, krows, :]
            for half in range(SWA_GROUP // 2):
                g0 = 2 * half * CHUNK
                out = (jnp.dot(probs[kv][g0:g0 + CHUNK], v0, preferred_element_type=F32)
                       + jnp.dot(probs[kv][g0 + CHUNK:g0 + 2 * CHUNK], v1, preferred_element_type=F32))
                c0 = (kv * SWA_GROUP + 2 * half) * head_dim
                o_ref[qrows, c0:c0 + pair] = out.astype(o_ref.dtype)
        return 0

    lax.fori_loop(0, n_chunks, chunk, 0, unroll=2)


def _swa(h, k_cache, v_cache, sinks, bias, *, n_seq, seq_len, q_col, k_col, v_col, q_width, kv_width, window):
    has_past = k_cache is not None
    rows = min(seq_len, 256)
    has_prev_block = seq_len > rows
    assert rows % CHUNK == 0 and (not has_prev_block or rows % window == 0)
    n_heads = sinks.shape[0]
    head_dim = q_width // n_heads
    n_kv = n_heads // SWA_GROUP
    n_k = window + CHUNK
    blocks_per_seq = seq_len // rows
    qb, kb, vb = q_col // q_width, k_col // kv_width, v_col // kv_width
    assert q_col % q_width == 0 and k_col % kv_width == 0 and v_col % kv_width == 0
    bias_g = bias.reshape(n_kv, SWA_GROUP * CHUNK, n_k)
    sink_g = jnp.repeat(sinks.astype(F32).reshape(n_kv, SWA_GROUP), CHUNK, axis=1)[..., None]

    in_specs = [
        pl.BlockSpec((rows, q_width), lambda b, r: (b * blocks_per_seq + r, qb)),
        pl.BlockSpec((rows, kv_width), lambda b, r: (b * blocks_per_seq + r, kb)),
        pl.BlockSpec((rows, kv_width), lambda b, r: (b * blocks_per_seq + r, vb)),
    ]
    args = [h, h, h]
    if has_prev_block:
        wpb = rows // window
        prev = lambda col: pl.BlockSpec(
            (window, kv_width), lambda b, r: (jnp.maximum((b * blocks_per_seq + r) * wpb - 1, 0), col))
        in_specs += [prev(kb), prev(vb)]
        args += [h, h]
    if has_past:
        cache = pl.BlockSpec((None, window, kv_width), lambda b, r: (b, 0, 0))
        in_specs += [cache, cache]
        args += [k_cache.reshape(n_seq, window, kv_width), v_cache.reshape(n_seq, window, kv_width)]
    in_specs += [_resident(bias_g.shape), _resident(sink_g.shape)]
    args += [bias_g, sink_g]

    return pl.pallas_call(
        functools.partial(_swa_body, rows=rows, window=window, n_kv=n_kv, head_dim=head_dim,
                          has_past=has_past, has_prev_block=has_prev_block),
        grid=(n_seq, blocks_per_seq),
        in_specs=in_specs,
        out_specs=pl.BlockSpec((rows, q_width), lambda b, r: (b * blocks_per_seq + r, 0)),
        out_shape=jax.ShapeDtypeStruct((n_seq * seq_len, q_width), BF16),
        scratch_shapes=[pltpu.VMEM((2, n_kv, window + rows, 2 * head_dim), BF16)] * 2,
        compiler_params=_params(("parallel", "arbitrary")),
        name="swa_attention",
    )(*args)


def _merge(x, y_ssm, attn, h, w_glu, w_o, w_out, *, gs_col, ga_col, tm):
    m, d = x.shape
    gcol = math.gcd(gs_col, ga_col, d)
    assert gcol % LANE == 0
    nsub = d // gcol

    def body(x_ref, y_ref, a_ref, *rest):
        gs = jnp.concatenate([r[...] for r in rest[:nsub]], axis=1)
        ga = jnp.concatenate([r[...] for r in rest[nsub:2 * nsub]], axis=1)
        wglu_ref, wo_ref, wout_ref, o_ref = rest[2 * nsub:]
        gl = jnp.dot(y_ref[...], wglu_ref[...], preferred_element_type=F32)
        ys = gl[:, :d] * jax.nn.sigmoid(gl[:, d:])
        ya = jnp.dot(a_ref[...], wo_ref[...], preferred_element_type=F32)
        merged = jax.nn.sigmoid(gs) * ys + jax.nn.sigmoid(ga) * ya
        o_ref[...] = x_ref[...] + jnp.dot(merged.astype(BF16), wout_ref[...], preferred_element_type=F32)

    gate_specs = [pl.BlockSpec((tm, gcol), lambda i, c=c: (i, c))
                  for base in (gs_col, ga_col) for c in range(base // gcol, base // gcol + nsub)]
    return pl.pallas_call(
        body,
        grid=(m // tm,),
        in_specs=[
            pl.BlockSpec((tm, d), lambda i: (i, 0)),
            pl.BlockSpec((tm, y_ssm.shape[1]), lambda i: (i, 0)),
            pl.BlockSpec((tm, attn.shape[1]), lambda i: (i, 0)),
            *gate_specs,
            _resident(w_glu.shape),
            _resident(w_o.shape),
            _resident(w_out.shape),
        ],
        out_specs=pl.BlockSpec((tm, d), lambda i: (i, 0)),
        out_shape=jax.ShapeDtypeStruct((m, d), F32),
        compiler_params=_params(("parallel",)),
        name="merge_out_proj",
    )(x, y_ssm, attn, *([h] * (2 * nsub)), w_glu, w_o, w_out)


def _xattn_body(x_ref, g_ref, mk_ref, mv_ref, wq_ref, wo_ref, o_ref, *, n_heads, head_dim):
    x = x_ref[...]
    q = jnp.dot(_rms(x, g_ref[...]).astype(BF16), wq_ref[...], preferred_element_type=F32).astype(BF16)
    scale = 1.0 / math.sqrt(head_dim)
    outs = []
    for hd in range(n_heads):
        hs = slice(hd * head_dim, (hd + 1) * head_dim)
        logits = lax.dot_general(q[:, hs], mk_ref[:, hs].astype(BF16), (((1,), (1,)), ((), ())),
                                 preferred_element_type=F32) * scale
        m = jnp.max(logits, axis=-1, keepdims=True)
        p = jnp.exp(logits - m)
        probs = p / jnp.sum(p, axis=-1, keepdims=True)
        outs.append(jnp.dot(probs.astype(BF16), mv_ref[:, hs].astype(BF16), preferred_element_type=F32))
    att = jnp.concatenate(outs, axis=1).astype(BF16)
    o_ref[...] = x + jnp.dot(att, wo_ref[...], preferred_element_type=F32)


def _xattn(x, g, mk, mv, w_q, w_o, *, n_seq, seq_len, n_heads):
    m, d = x.shape
    tm = min(seq_len, 256)
    blocks_per_seq = seq_len // tm
    mem_len, xw = mk.shape[1], mk.shape[2]
    return pl.pallas_call(
        functools.partial(_xattn_body, n_heads=n_heads, head_dim=xw // n_heads),
        grid=(n_seq, blocks_per_seq),
        in_specs=[
            pl.BlockSpec((tm, d), lambda b, r: (b * blocks_per_seq + r, 0)),
            _resident((1, d)),
            pl.BlockSpec((None, mem_len, xw), lambda b, r: (b, 0, 0)),
            pl.BlockSpec((None, mem_len, xw), lambda b, r: (b, 0, 0)),
            _resident(w_q.shape),
            _resident(w_o.shape),
        ],
        out_specs=pl.BlockSpec((tm, d), lambda b, r: (b * blocks_per_seq + r, 0)),
        out_shape=jax.ShapeDtypeStruct((m, d), F32),
        compiler_params=_params(("parallel", "arbitrary")),
        name="cross_attention",
    )(x, g.reshape(1, d), mk, mv, w_q, w_o)


def _router_body(xp_ref, xs_ref, g_ref, wr_ref, br_ref, xn_ref, idx_ref, gate_ref, rank_ref, cnt_ref, carry,
                 *, n_exp, n_prompt_tiles):
    i = pl.program_id(0)

    @pl.when(i == 0)
    def _():
        carry[...] = jnp.zeros_like(carry)

    xn = _rms(jnp.where(i < n_prompt_tiles, xp_ref[...], xs_ref[...]), g_ref[...])
    xn_ref[...] = xn
    tm = xn.shape[0]
    logits = lax.dot_general(wr_ref[...], xn, (((1,), (1,)), ((), ())),
                             precision=lax.Precision.HIGHEST, preferred_element_type=F32) + br_ref[...]
    eid = lax.broadcasted_iota(jnp.int32, (n_exp, tm), 0)
    vals, hots = [], []
    work = logits
    for k in range(TOP_K):
        m = jnp.max(work, axis=0, keepdims=True)
        sel = jnp.min(jnp.where(work == m, eid, n_exp), axis=0, keepdims=True)
        hot = eid == sel
        vals.append(m)
        hots.append(hot)
        idx_ref[k:k + 1, :] = sel
        work = jnp.where(hot, -jnp.inf, work)
    ex = [jnp.exp(v - vals[0]) for v in vals]
    tot = ex[0] + ex[1] + ex[2] + ex[3]
    for k in range(TOP_K):
        gate_ref[k:k + 1, :] = ex[k] / tot

    chosen = (hots[0] | hots[1] | hots[2] | hots[3])
    a = jnp.where(chosen, 1.0, 0.0)
    earlier = (lax.broadcasted_iota(jnp.int32, (tm, tm), 0) < lax.broadcasted_iota(jnp.int32, (tm, tm), 1))
    prefix = jnp.dot(a.astype(BF16), jnp.where(earlier, 1.0, 0.0).astype(BF16),
                     preferred_element_type=F32) + carry[...]
    for k in range(TOP_K):
        rank_ref[k:k + 1, :] = jnp.sum(jnp.where(hots[k], prefix, 0.0), axis=0, keepdims=True).astype(jnp.int32)
    carry[...] = carry[...] + jnp.sum(a, axis=1, keepdims=True)
    cnt_ref[...] = carry[...].astype(jnp.int32)


def _two_group_specs(tm, d, npt):
    return [pl.BlockSpec((tm, d), lambda i: (jnp.minimum(i, npt - 1), 0)),
            pl.BlockSpec((tm, d), lambda i: (jnp.maximum(i - npt, 0), 0))]


def _router(x_p, x_s, g, w_router, b_router):
    d = x_p.shape[1]
    t = x_p.shape[0] + x_s.shape[0]
    n_exp = w_router.shape[1]
    tm = ROUTER_TM
    assert x_p.shape[0] % tm == 0 and x_s.shape[0] % tm == 0
    npt = x_p.shape[0] // tm
    row = pl.BlockSpec((TOP_K, tm), lambda i: (0, i))
    return pl.pallas_call(
        functools.partial(_router_body, n_exp=n_exp, n_prompt_tiles=npt),
        grid=(t // tm,),
        in_specs=[
            *_two_group_specs(tm, d, npt),
            _resident((1, d)),
            _resident((n_exp, d)),
            _resident((n_exp, 1)),
        ],
        out_specs=[pl.BlockSpec((tm, d), lambda i: (i, 0)), row, row, row,
                   pl.BlockSpec((n_exp, 1), lambda i: (0, 0))],
        out_shape=[
            jax.ShapeDtypeStruct((t, d), F32),
            jax.ShapeDtypeStruct((TOP_K, t), jnp.int32),
            jax.ShapeDtypeStruct((TOP_K, t), F32),
            jax.ShapeDtypeStruct((TOP_K, t), jnp.int32),
            jax.ShapeDtypeStruct((n_exp, 1), jnp.int32),
        ],
        scratch_shapes=[pltpu.VMEM((n_exp, 1), F32)],
        compiler_params=_params(("arbitrary",)),
        name="moe_router",
    )(x_p, x_s, g.reshape(1, d), w_router.astype(F32).T, b_router.astype(F32).reshape(n_exp, 1))


def _dispatch_body(pos_ref, fill_ref, x_ref, xs_ref, zeros, sem, fill_sem, *, n_exp):
    tm, d = x_ref.shape
    fill_rows = zeros.shape[0]

    def fill_copy(e):
        start = pl.multiple_of(fill_ref[e], SUBLANE)
        return pltpu.make_async_copy(zeros, xs_ref.at[pl.ds(start, fill_rows), :], fill_sem)

    @pl.when(pl.program_id(0) == 0)
    def _():
        zeros[...] = jnp.zeros_like(zeros)
        for e in range(n_exp):
            fill_copy(e).start()
        for e in range(n_exp):
            fill_copy(e).wait()

    def issue(t, _):
        for k in range(TOP_K):
            pltpu.make_async_copy(x_ref.at[pl.ds(t, 1), :], xs_ref.at[pl.ds(pos_ref[k, t], 1), :], sem).start()
        return 0

    lax.fori_loop(0, tm, issue, 0, unroll=8)
    for k in range(TOP_K):
        pltpu.make_async_copy(x_ref, xs_ref.at[pl.ds(0, tm), :], sem).wait()


def _dispatch(xn, pos, fill_start, n_rows):
    t, d = xn.shape
    tm = TOK_TM
    n_exp = fill_start.shape[0]
    fill_rows = MOE_SB + SUBLANE
    return pl.pallas_call(
        functools.partial(_dispatch_body, n_exp=n_exp),
        grid=(t // tm,),
        in_specs=[
            pl.BlockSpec((TOP_K, tm), lambda i: (0, i), memory_space=pltpu.SMEM),
            pl.BlockSpec(memory_space=pltpu.SMEM),
            pl.BlockSpec((tm, d), lambda i: (i, 0)),
        ],
        out_specs=pl.BlockSpec(memory_space=pl.ANY),
        out_shape=jax.ShapeDtypeStruct((n_rows + fill_rows, d), xn.dtype),
        scratch_shapes=[pltpu.VMEM((fill_rows, d), xn.dtype), pltpu.SemaphoreType.DMA, pltpu.SemaphoreType.DMA],
        compiler_params=_params(("arbitrary",)),
        name="moe_dispatch",
    )(pos, fill_start, xn)


def _experts_body(te_ref, row0_ref, nv_ref, xs_ref, wg_ref, wl_ref, bg_ref, bl_ref, wd_ref, bd_ref,
                  ys_ref, acc, xb, stage, wgb, wlb, wdb, in_sem, out_sem, *, n_j, balanced_sub):
    del te_ref
    i, j = pl.program_id(0), pl.program_id(1)
    n_valid = nv_ref[i]
    row0 = row0_ref[i]
    d = acc.shape[1]
    n_sub = (n_valid + MOE_SB - 1) // MOE_SB
    n_big = n_valid // MOE_BIG
    small0 = n_big * MOE_BIG
    n_small = n_sub - n_big * (MOE_BIG // MOE_SB)

    def in_copy(s, slot):
        src = pl.ds(pl.multiple_of(row0 + s * MOE_SB, MOE_SB), MOE_SB)
        return pltpu.make_async_copy(xs_ref.at[src, :], stage.at[slot], in_sem.at[slot])

    def out_copy(first, size):
        src = pl.ds(pl.multiple_of(first, MOE_SB), size)
        dst = pl.ds(pl.multiple_of(row0 + first, MOE_SB), size)
        return pltpu.make_async_copy(acc.at[src, :], ys_ref.at[dst, :], out_sem)

    def drain(blocks):
        def one(s, _):
            out_copy(0, MOE_SB).wait()
            return 0

        lax.fori_loop(0, blocks, one, 0)

    n_tiles = pl.num_programs(0)
    has_next = (i + 1 < n_tiles) & (nv_ref[jnp.minimum(i + 1, n_tiles - 1)] > 0)
    prev_sub = (nv_ref[jnp.maximum(i - 1, 0)] + MOE_SB - 1) // MOE_SB

    def compute(first, size):
        rows = pl.ds(pl.multiple_of(first, MOE_SB), size)
        xs = xb[rows, :]
        hg = jnp.dot(xs, wgb[...], preferred_element_type=F32) + bg_ref[...]
        hl = jnp.dot(xs, wlb[...], preferred_element_type=F32) + bl_ref[...]
        glu = jnp.minimum(hg, SWIGLU_LIMIT)
        lin = jnp.clip(hl, -SWIGLU_LIMIT, SWIGLU_LIMIT)
        act = (glu * jax.nn.sigmoid(SWIGLU_ALPHA * glu) * (lin + 1.0)).astype(BF16)
        for c in range(0, d, MOE_DOWN_COLS):
            cols = slice(c, c + MOE_DOWN_COLS)
            acc[rows, cols] += jnp.dot(act, wdb[:, cols], preferred_element_type=F32)

        @pl.when(j == n_j - 1)
        def _():
            out_copy(first, size).start()

    @pl.when(n_valid > 0)
    def _():
        @pl.when(j == 0)
        def _():
            in_copy(0, 0).start()

            @pl.when(i > 0)
            def _():
                drain(prev_sub)

            def load(s, _):
                slot = s % 2
                rows = pl.ds(pl.multiple_of(s * MOE_SB, MOE_SB), MOE_SB)

                @pl.when(s + 1 < n_sub)
                def _():
                    in_copy(s + 1, 1 - slot).start()

                in_copy(s, slot).wait()
                xb[rows, :] = stage[slot].astype(BF16)
                acc[rows, :] = jnp.broadcast_to(bd_ref[...], (MOE_SB, d))
                return 0

            lax.fori_loop(0, n_sub, load, 0)

        wgb[...] = wg_ref[...].astype(BF16)
        wlb[...] = wl_ref[...].astype(BF16)
        wdb[...] = wd_ref[...].astype(BF16)

        def big(s, _):
            compute(s * MOE_BIG, MOE_BIG)
            return 0

        def small(s, _):
            compute(small0 + s * MOE_SB, MOE_SB)
            return 0

        balanced = n_sub == balanced_sub

        @pl.when(balanced)
        def _():
            compute(0, balanced_sub * MOE_SB)

        @pl.when(jnp.logical_not(balanced))
        def _():
            lax.fori_loop(0, n_big, big, 0)
            lax.fori_loop(0, n_small, small, 0)

        @pl.when((j == n_j - 1) & jnp.logical_not(has_next))
        def _():
            drain(n_sub)


def _experts(xs, tile_expert, tile_row0, tile_valid, n_rows, n_assign, w_up, b_up, w_down, b_down):
    d = xs.shape[1]
    n_tiles = tile_expert.shape[0]
    depth, n_exp, _, ff2 = w_up.shape
    assert depth == 1
    balanced_sub = min(-(-n_assign // (n_exp * MOE_SB)), MOE_CAP // MOE_SB)
    d_ff = ff2 // 2
    n_j = d_ff // MOE_TF
    b_up = b_up.reshape(depth, n_exp, 1, ff2)
    b_down = b_down.reshape(depth, n_exp, 1, d)

    def jsel(i, j, nv):
        return jnp.where(nv[i] > 0, j, n_j - 1)

    grid_spec = pltpu.PrefetchScalarGridSpec(
        num_scalar_prefetch=3,
        grid=(n_tiles, n_j),
        in_specs=[
            pl.BlockSpec(memory_space=pl.ANY),
            pl.BlockSpec((None, None, d, MOE_TF), lambda i, j, te, r0, nv: (0, te[i], 0, jsel(i, j, nv))),
            pl.BlockSpec((None, None, d, MOE_TF), lambda i, j, te, r0, nv: (0, te[i], 0, n_j + jsel(i, j, nv))),
            pl.BlockSpec((None, None, 1, MOE_TF), lambda i, j, te, r0, nv: (0, te[i], 0, jsel(i, j, nv))),
            pl.BlockSpec((None, None, 1, MOE_TF), lambda i, j, te, r0, nv: (0, te[i], 0, n_j + jsel(i, j, nv))),
            pl.BlockSpec((None, None, MOE_TF, d), lambda i, j, te, r0, nv: (0, te[i], jsel(i, j, nv), 0)),
            pl.BlockSpec((None, None, 1, d), lambda i, j, te, r0, nv: (0, te[i], 0, 0)),
        ],
        out_specs=pl.BlockSpec(memory_space=pl.ANY),
        scratch_shapes=[
            pltpu.VMEM((MOE_CAP, d), F32),
            pltpu.VMEM((MOE_CAP, d), BF16),
            pltpu.VMEM((2, MOE_SB, d), F32),
            pltpu.VMEM((d, MOE_TF), BF16),
            pltpu.VMEM((d, MOE_TF), BF16),
            pltpu.VMEM((MOE_TF, d), BF16),
            pltpu.SemaphoreType.DMA((2,)),
            pltpu.SemaphoreType.DMA,
        ],
    )
    return pl.pallas_call(
        functools.partial(_experts_body, n_j=n_j, balanced_sub=balanced_sub),
        grid_spec=grid_spec,
        out_shape=jax.ShapeDtypeStruct((n_rows, d), F32),
        compiler_params=_params(("arbitrary", "arbitrary")),
        name="moe_experts",
    )(tile_expert, tile_row0, tile_valid, xs, w_up, w_up, b_up, b_up, w_down, b_down)


def _combine_body(pos_ref, nxt_ref, xp_ref, xs_ref, gate_ref, g_ref, ys_ref, op_ref, os_ref, buf, sem,
                  *, n_prompt_tiles):
    i = pl.program_id(0)
    n = pl.num_programs(0)
    tm = xp_ref.shape[0]

    def gather(p_ref, slot):
        def issue(t, _):
            for k in range(TOP_K):
                pltpu.make_async_copy(ys_ref.at[pl.ds(p_ref[k, t], 1), :], buf.at[slot, k, pl.ds(t, 1), :],
                                      sem.at[slot]).start()
            return 0

        lax.fori_loop(0, tm, issue, 0, unroll=8)

    slot = i % 2

    @pl.when(i == 0)
    def _():
        gather(pos_ref, 0)

    @pl.when(i + 1 < n)
    def _():
        gather(nxt_ref, 1 - slot)

    for k in range(TOP_K):
        pltpu.make_async_copy(ys_ref.at[pl.ds(0, tm), :], buf.at[slot, k], sem.at[slot]).wait()

    out = jnp.where(i < n_prompt_tiles, xp_ref[...], xs_ref[...])
    for k in range(TOP_K):
        out = out + gate_ref[:, k:k + 1] * buf[slot, k]
    y = _rms(out, g_ref[...])

    @pl.when(i < n_prompt_tiles)
    def _():
        op_ref[...] = y

    @pl.when(i >= n_prompt_tiles)
    def _():
        os_ref[...] = y


def _combine(x_p, x_s, ys, pos, gate_t, g_final):
    n_prompt, d = x_p.shape
    t = n_prompt + x_s.shape[0]
    tm = TOK_TM
    assert n_prompt % tm == 0 and t % tm == 0
    npt = n_prompt // tm
    n = t // tm
    return pl.pallas_call(
        functools.partial(_combine_body, n_prompt_tiles=npt),
        grid=(n,),
        in_specs=[
            pl.BlockSpec((TOP_K, tm), lambda i: (0, i), memory_space=pltpu.SMEM),
            pl.BlockSpec((TOP_K, tm), lambda i: (0, jnp.minimum(i + 1, n - 1)), memory_space=pltpu.SMEM),
            *_two_group_specs(tm, d, npt),
            pl.BlockSpec((tm, TOP_K), lambda i: (i, 0)),
            _resident((1, d)),
            pl.BlockSpec(memory_space=pl.ANY),
        ],
        out_specs=_two_group_specs(tm, d, npt),
        out_shape=[
            jax.ShapeDtypeStruct((n_prompt, d), F32),
            jax.ShapeDtypeStruct((t - n_prompt, d), F32),
        ],
        scratch_shapes=[pltpu.VMEM((2, TOP_K, tm, d), F32), pltpu.SemaphoreType.DMA((2,))],
        compiler_params=_params(("arbitrary",)),
        name="moe_combine",
    )(pos, pos, x_p, x_s, gate_t, g_final.reshape(1, d), ys)


def _moe_layout(idx, rank, counts, n_tiles):
    n_exp = counts.shape[0]
    region = ((counts + MOE_SB - 1) // MOE_SB) * MOE_SB
    row_start = jnp.cumsum(region) - region
    tiles_per_exp = (counts + MOE_CAP - 1) // MOE_CAP
    tile_end = jnp.cumsum(tiles_per_exp)
    tile_start = tile_end - tiles_per_exp
    n_used = tile_end[-1]
    tiles = jnp.arange(n_tiles, dtype=jnp.int32)
    te = jnp.minimum(jnp.searchsorted(tile_end, tiles, side="right"), n_exp - 1).astype(jnp.int32)
    te = jnp.where(tiles < n_used, te, te[jnp.maximum(n_used - 1, 0)])
    first = (tiles - tile_start[te]) * MOE_CAP
    valid = jnp.where(tiles < n_used, jnp.clip(counts[te] - first, 0, MOE_CAP), 0)
    row0 = jnp.where(tiles < n_used, row_start[te] + first, 0)
    hot = idx[..., None] == jnp.arange(n_exp, dtype=jnp.int32)
    pos = rank + jnp.sum(jnp.where(hot, row_start, 0), axis=-1)
    fill_start = row_start + (counts // SUBLANE) * SUBLANE
    i32 = lambda a: a.astype(jnp.int32)
    return i32(pos), te, i32(row0), i32(valid), i32(fill_start)


def kernel(x_prompt, x_sample, state_ssm_re, state_ssm_im, cache_swa_k, cache_swa_v, cache_mem_k, cache_mem_v, mem_prompt, w_in, ssm_lam_re, ssm_lam_im, ssm_log_dt, ssm_b_re, ssm_b_im, ssm_c_re, ssm_c_im, ssm_d, w_glu, swa_sinks, rel_bias, w_o_attn, w_out, w_xq, w_mk, w_mv, w_xo, w_router, b_router, w_up, b_up, w_down, b_down, norm_mix, norm_x, norm_mem, norm_moe, norm_final):
    depth = w_in.shape[0]
    assert depth == 1, "single-layer step"
    bp, lp, d = x_prompt.shape
    bs, ls, _ = x_sample.shape
    n_p, n_s = bp * lp, bs * ls
    t = n_p + n_s
    _, _, window, kvh, hd = cache_swa_k.shape
    kv_width = kvh * hd
    n_heads = swa_sinks.shape[1]
    q_width = n_heads * hd
    ssm_width = ssm_d.shape[1]
    _, _, mem_len, x_heads, x_hd = cache_mem_k.shape
    xw = x_heads * x_hd
    ssm_groups, ssm_state = ssm_lam_re.shape[1], ssm_lam_re.shape[2]
    q_col = ssm_width
    k_col = q_col + q_width
    v_col = k_col + kv_width
    gs_col = v_col + kv_width
    ga_col = gs_col + d

    l = 0
    w_in_b = w_in[l].astype(BF16)
    w_glu_b, w_o_b, w_out_b = w_glu[l].astype(BF16), w_o_attn[l].astype(BF16), w_out[l].astype(BF16)
    w_xq_b, w_xo_b = w_xq[l].astype(BF16), w_xo[l].astype(BF16)
    w_mkv_b = jnp.concatenate([w_mk[l], w_mv[l]], axis=1).astype(BF16)
    a_re, a_im, bcat, ccat = _s5_discretize(ssm_lam_re[l], ssm_lam_im[l], ssm_log_dt[l], ssm_b_re[l],
                                            ssm_b_im[l], ssm_c_re[l], ssm_c_im[l])
    bucket = _t5_bucket_table(CHUNK, window + CHUNK, window)
    bias = jnp.transpose(rel_bias.astype(F32)[bucket], (2, 0, 1))

    def mixers(x, n_seq, seq_len, s0_re, s0_im, k_cache, v_cache):
        xf = x.reshape(n_seq * seq_len, d)
        h = _norm_matmul(xf, norm_mix[l], w_in_b, tm=1024, tn=1664)
        y_ssm, s_re, s_im = _s5(h[:, :ssm_width], s0_re, s0_im, a_re, a_im, bcat, ccat, ssm_d[l],
                                n_seq=n_seq, seq_len=seq_len)
        attn = _swa(h, k_cache, v_cache, swa_sinks[l], bias, n_seq=n_seq, seq_len=seq_len, q_col=q_col,
                    k_col=k_col, v_col=v_col, q_width=q_width, kv_width=kv_width, window=window)
        x1 = _merge(xf, y_ssm, attn, h, w_glu_b, w_o_b, w_out_b, gs_col=gs_col, ga_col=ga_col, tm=256)
        h3 = h.reshape(n_seq, seq_len, -1)
        k_new = h3[:, :, k_col:k_col + kv_width]
        v_new = h3[:, :, v_col:v_col + kv_width]
        if k_cache is not None:
            k_new = jnp.concatenate([k_cache.reshape(n_seq, window, kv_width), k_new], axis=1)
            v_new = jnp.concatenate([v_cache.reshape(n_seq, window, kv_width), v_new], axis=1)
        k_new = k_new[:, -window:].reshape(1, n_seq, window, kvh, hd)
        v_new = v_new[:, -window:].reshape(1, n_seq, window, kvh, hd)
        state_shape = (1, n_seq, ssm_groups, ssm_state)
        return x1, s_re.reshape(state_shape), s_im.reshape(state_shape), k_new, v_new

    zero_state = jnp.zeros((bp, ssm_groups * ssm_state), F32)
    x1_p, sre_p, sim_p, k_p, v_p = mixers(x_prompt, bp, lp, zero_state, zero_state, None, None)
    x1_s, sre_s, sim_s, k_s, v_s = mixers(x_sample, bs, ls, state_ssm_re[l], state_ssm_im[l],
                                          cache_swa_k[l], cache_swa_v[l])

    mkv = _norm_matmul(mem_prompt.reshape(bp * mem_len, d), norm_mem[l], w_mkv_b, tm=1024, tn=512)
    mkv = mkv.reshape(bp, mem_len, 2 * xw)
    mk_p, mv_p = mkv[:, :, :xw], mkv[:, :, xw:]

    x2_p = _xattn(x1_p, norm_x[l], mk_p, mv_p, w_xq_b, w_xo_b, n_seq=bp, seq_len=lp, n_heads=x_heads)
    x2_s = _xattn(x1_s, norm_x[l], cache_mem_k[l].reshape(bs, mem_len, xw), cache_mem_v[l].reshape(bs, mem_len, xw),
                  w_xq_b, w_xo_b, n_seq=bs, seq_len=ls, n_heads=x_heads)

    n_exp = w_router.shape[2]
    xn, idx, gate, rank, counts = _router(x2_p, x2_s, norm_moe[l], w_router[l], b_router[l])
    n_tiles = (t * TOP_K) // MOE_CAP + n_exp
    n_rows = t * TOP_K + n_exp * MOE_SB
    pos, tile_expert, tile_row0, tile_valid, fill_start = _moe_layout(idx, rank, counts[:, 0], n_tiles)
    xs = _dispatch(xn, pos, fill_start, n_rows)
    ys = _experts(xs, tile_expert, tile_row0, tile_valid, n_rows, t * TOP_K, w_up, b_up, w_down, b_down)
    y_p, y_s = _combine(x2_p, x2_s, ys, pos, gate.T, norm_final)

    return (y_p.reshape(bp, lp, d), y_s.reshape(bs, ls, d), sre_p, sim_p, k_p, v_p,
            mk_p.reshape(1, bp, mem_len, x_heads, x_hd), mv_p.reshape(1, bp, mem_len, x_heads, x_hd),
            sre_s, sim_s, k_s, v_s)
```

```python
import functools
import math

import numpy as np
import jax
import jax.numpy as jnp
from jax import lax
from jax.experimental import pallas as pl
from jax.experimental.pallas import tpu as pltpu

F32 = jnp.float32
BF16 = jnp.bfloat16

LANE = 128
SUBLANE = 8
VMEM_LIMIT = 56 * 1024 * 1024

CHUNK = 64
SSM_GROUP = 16
SWA_GROUP = 4
REL_BUCKETS = 32
REL_MAX_DIST = 128
TOP_K = 4
SWIGLU_LIMIT = 7.0
SWIGLU_ALPHA = 1.702
RMS_EPS = 1e-5
SEQ_PER_SCAN = SUBLANE

S5_STRIP = 512
MOE_CAP = 2560
MOE_SB = 256
MOE_BIG = 1024
MOE_DOWN_COLS = 512
MOE_TF = 256
TOK_TM = 256
DISPATCH_TM = 512
ROUTER_TM = 512


def _params(sem, vmem=VMEM_LIMIT):
    return pltpu.CompilerParams(dimension_semantics=sem, vmem_limit_bytes=vmem)


def _rms(x, g):
    return x * lax.rsqrt(jnp.mean(x * x, axis=-1, keepdims=True) + RMS_EPS) * g


def _resident(shape):
    nd = len(shape)
    return pl.BlockSpec(shape, lambda *_: (0,) * nd, pipeline_mode=pl.Buffered(1))


def _norm_matmul_body(x_ref, g_ref, w_ref, o_ref, xn_ref):
    @pl.when(pl.program_id(1) == 0)
    def _():
        xn_ref[...] = _rms(x_ref[...], g_ref[...]).astype(BF16)

    o_ref[...] = jnp.dot(xn_ref[...], w_ref[...], preferred_element_type=F32)


def _norm_matmul(x, g, w, *, tm, tn):
    m, d = x.shape
    n = w.shape[1]
    tm = min(tm, m)
    assert m % tm == 0 and n % tn == 0
    return pl.pallas_call(
        _norm_matmul_body,
        grid=(m // tm, n // tn),
        in_specs=[
            pl.BlockSpec((tm, d), lambda i, j: (i, 0)),
            pl.BlockSpec((1, d), lambda i, j: (0, 0)),
            pl.BlockSpec((d, tn), lambda i, j: (0, j)),
        ],
        out_specs=pl.BlockSpec((tm, tn), lambda i, j: (i, j)),
        out_shape=jax.ShapeDtypeStruct((m, n), F32),
        scratch_shapes=[pltpu.VMEM((tm, d), BF16)],
        compiler_params=_params(("parallel", "arbitrary")),
        name="norm_matmul",
    )(x, g.reshape(1, d), w)


def _s5_discretize(lam_re, lam_im, log_dt, b_re, b_im, c_re, c_im):
    g, p = lam_re.shape
    gpb = LANE // SSM_GROUP
    nblk = g // gpb
    lr, li = lam_re.astype(F32), lam_im.astype(F32)
    dt = jnp.exp(log_dt.astype(F32))[:, None]
    mag = jnp.exp(lr * dt)
    a_re = mag * jnp.cos(li * dt)
    a_im = mag * jnp.sin(li * dt)
    den = lr * lr + li * li
    f_re = ((a_re - 1.0) * lr + a_im * li) / den
    f_im = (a_im * lr - (a_re - 1.0) * li) / den
    br, bi = b_re.astype(F32), b_im.astype(F32)
    bb_re = f_re[..., None] * br - f_im[..., None] * bi
    bb_im = f_re[..., None] * bi + f_im[..., None] * br
    eye = jnp.eye(gpb, dtype=F32)

    def b_blocks(bb):
        t = jnp.transpose(bb, (0, 2, 1)).reshape(nblk, gpb, SSM_GROUP, p)
        return jnp.einsum("jghp,gk->jghkp", t, eye).reshape(nblk, gpb * SSM_GROUP, gpb * p)

    def c_blocks(c):
        t = c.astype(F32).reshape(nblk, gpb, SSM_GROUP, p)
        return jnp.einsum("jghp,gk->jgpkh", t, eye).reshape(nblk, gpb * p, gpb * SSM_GROUP)

    bcat = jnp.concatenate([b_blocks(bb_re), b_blocks(bb_im)], axis=2).astype(BF16)
    ccat = jnp.concatenate([c_blocks(c_re), -c_blocks(c_im)], axis=1)
    zero = jnp.zeros_like(ccat[0::2])
    ccat = jnp.concatenate([jnp.concatenate([ccat[0::2], zero], axis=2),
                            jnp.concatenate([zero, ccat[1::2]], axis=2)], axis=1).astype(BF16)
    bro = lambda a: jnp.broadcast_to(a.reshape(1, g * p), (SUBLANE, g * p))
    return bro(a_re), bro(a_im), bcat, ccat


def _s5_body(u_ref, s0re_ref, s0im_ref, are_ref, aim_ref, bcat_ref, ccat_ref, d_ref,
             y_ref, ore_ref, oim_ref, sre, sim, st_re, st_im, *, tc, nblk, blk_state):
    c = pl.program_id(1)

    @pl.when(c == 0)
    def _():
        st_re[...] = s0re_ref[...]
        st_im[...] = s0im_ref[...]

    for j in range(nblk):
        ub = u_ref[:, j * LANE:(j + 1) * LANE].astype(BF16)
        bu = jnp.dot(ub, bcat_ref[j], preferred_element_type=F32)
        sre[:, j * blk_state:(j + 1) * blk_state] = bu[:, :blk_state]
        sim[:, j * blk_state:(j + 1) * blk_state] = bu[:, blk_state:]

    n_state = sre.shape[1]
    for strip in range(n_state // S5_STRIP):
        sl = slice(strip * S5_STRIP, (strip + 1) * S5_STRIP)
        ar = are_ref[:, sl]
        ai = aim_ref[:, sl]

        def step(t, carry, sl=sl, ar=ar, ai=ai):
            sr, si = carry
            rows = pl.ds(pl.multiple_of(t * SUBLANE, SUBLANE), SUBLANE)
            nr = ar * sr - ai * si + sre[rows, sl]
            ni = ar * si + ai * sr + sim[rows, sl]
            sre[rows, sl] = nr
            sim[rows, sl] = ni
            return nr, ni

        sr, si = lax.fori_loop(0, tc, step, (st_re[:, sl], st_im[:, sl]), unroll=4)
        st_re[:, sl] = sr
        st_im[:, sl] = si

    for jj in range(nblk // 2):
        ca = slice(2 * jj * blk_state, (2 * jj + 1) * blk_state)
        cb = slice((2 * jj + 1) * blk_state, (2 * jj + 2) * blk_state)
        scat = jnp.concatenate([sre[:, ca].astype(BF16), sim[:, ca].astype(BF16),
                                sre[:, cb].astype(BF16), sim[:, cb].astype(BF16)], axis=1)
        yj = jnp.dot(scat, ccat_ref[jj], preferred_element_type=F32)
        lanes = slice(2 * jj * LANE, (2 * jj + 2) * LANE)
        yj = yj + d_ref[:, lanes] * u_ref[:, lanes]
        y_ref[:, lanes] = jax.nn.gelu(yj).astype(y_ref.dtype)

    @pl.when(c == pl.num_programs(1) - 1)
    def _():
        ore_ref[...] = st_re[...]
        oim_ref[...] = st_im[...]


def _s5(u, s0_re, s0_im, a_re, a_im, bcat, ccat, d_skip, *, n_seq, seq_len):
    w = u.shape[1]
    n_state = a_re.shape[1]
    nblk = bcat.shape[0]
    ng = n_seq // SEQ_PER_SCAN
    tc = min(seq_len, 32)
    rows = tc * SEQ_PER_SCAN
    u_tb = u.reshape(ng, SEQ_PER_SCAN, seq_len, w).transpose(0, 2, 1, 3).reshape(ng, seq_len * SEQ_PER_SCAN, w)
    s0_re = s0_re.astype(F32).reshape(ng, SEQ_PER_SCAN, n_state)
    s0_im = s0_im.astype(F32).reshape(ng, SEQ_PER_SCAN, n_state)
    state_spec = pl.BlockSpec((None, SEQ_PER_SCAN, n_state), lambda g, c: (g, 0, 0))
    y_tb, o_re, o_im = pl.pallas_call(
        functools.partial(_s5_body, tc=tc, nblk=nblk, blk_state=n_state // nblk),
        grid=(ng, seq_len // tc),
        in_specs=[
            pl.BlockSpec((None, rows, w), lambda g, c: (g, c, 0)),
            state_spec,
            state_spec,
            _resident(a_re.shape),
            _resident(a_im.shape),
            _resident(bcat.shape),
            _resident(ccat.shape),
            _resident((1, w)),
        ],
        out_specs=[
            pl.BlockSpec((None, rows, w), lambda g, c: (g, c, 0)),
            state_spec,
            state_spec,
        ],
        out_shape=[
            jax.ShapeDtypeStruct((ng, seq_len * SEQ_PER_SCAN, w), BF16),
            jax.ShapeDtypeStruct((ng, SEQ_PER_SCAN, n_state), F32),
            jax.ShapeDtypeStruct((ng, SEQ_PER_SCAN, n_state), F32),
        ],
        scratch_shapes=[
            pltpu.VMEM((rows, n_state), F32),
            pltpu.VMEM((rows, n_state), F32),
            pltpu.VMEM((SEQ_PER_SCAN, n_state), F32),
            pltpu.VMEM((SEQ_PER_SCAN, n_state), F32),
        ],
        compiler_params=_params(("parallel", "arbitrary")),
        name="s5_scan",
    )(u_tb, s0_re, s0_im, a_re, a_im, bcat, ccat, d_skip.astype(F32).reshape(1, w))
    y = y_tb.reshape(ng, seq_len, SEQ_PER_SCAN, w).transpose(0, 2, 1, 3).reshape(n_seq * seq_len, w)
    return y, o_re.reshape(n_seq, n_state), o_im.reshape(n_seq, n_state)


def _t5_bucket_table(n_q, n_k, n_prev):
    half = REL_BUCKETS // 2
    max_exact = half // 2
    assert (REL_MAX_DIST // max_exact) ** 2 == 2 ** (half - max_exact)
    rel = np.arange(n_k)[None, :] - n_prev - np.arange(n_q)[:, None]
    n = np.abs(rel)
    k = np.zeros_like(n)
    for kk in range(1, 2 * half):
        k = np.where(n * n >= max_exact * max_exact * (2 ** kk), kk, k)
    large = np.minimum(max_exact + k, half - 1)
    return ((rel > 0) * half + np.where(n < max_exact, n, large)).astype(np.int32)


def _swa_body(*refs, rows, window, n_kv, head_dim, has_past, has_prev_block):
    it = iter(refs)
    q_ref, kc_ref, vc_ref = next(it), next(it), next(it)
    kp_ref = vp_ref = kcache_ref = vcache_ref = None
    if has_prev_block:
        kp_ref, vp_ref = next(it), next(it)
    if has_past:
        kcache_ref, vcache_ref = next(it), next(it)
    bias_ref, sink_ref = next(it), next(it)
    o_ref = next(it)
    kx, vx = next(it), next(it)

    r = pl.program_id(1)
    n_k = window + CHUNK
    pair = 2 * head_dim

    def put(dst, lo, src):
        n = src.shape[0]
        zero = jnp.zeros((n, head_dim), F32)
        for kv in range(n_kv):
            x = src[:, kv * head_dim:(kv + 1) * head_dim]
            dst[0, kv, lo:lo + n, :] = jnp.concatenate([x, zero], axis=1).astype(BF16)
            dst[1, kv, lo:lo + n, :] = jnp.concatenate([zero, x], axis=1).astype(BF16)

    @pl.when(r == 0)
    def _():
        if has_past:
            put(kx, 0, kcache_ref[...])
            put(vx, 0, vcache_ref[...])
        else:
            kx[:, :, 0:window, :] = jnp.zeros((2, n_kv, window, pair), BF16)
            vx[:, :, 0:window, :] = jnp.zeros((2, n_kv, window, pair), BF16)

    if has_prev_block:
        @pl.when(r > 0)
        def _():
            put(kx, 0, kp_ref[...])
            put(vx, 0, vp_ref[...])

    put(kx, window, kc_ref[...])
    put(vx, window, vc_ref[...])

    scale = 1.0 / math.sqrt(head_dim)
    n_chunks = rows // CHUNK
    key_pos = lax.broadcasted_iota(jnp.int32, (SWA_GROUP * CHUNK, n_k), 1)
    nt = lambda a, b: lax.dot_general(a, b, (((1,), (1,)), ((), ())), preferred_element_type=F32)

    def chunk(cc, _):
        q0 = pl.multiple_of(cc * CHUNK, CHUNK)
        qrows = pl.ds(q0, CHUNK)
        krows = pl.ds(q0, n_k)
        if not has_past:
            valid = key_pos + (r * n_chunks + cc) * CHUNK >= window
        logits = []
        for kv in range(n_kv):
            k0, k1 = kx[0, kv, krows, :], kx[1, kv, krows, :]
            parts = []
            for half in range(SWA_GROUP // 2):
                c0 = (kv * SWA_GROUP + 2 * half) * head_dim
                qp = (q_ref[qrows, c0:c0 + pair] * scale).astype(BF16)
                parts += [nt(qp, k0), nt(qp, k1)]
            lg = jnp.concatenate(parts, axis=0) + bias_ref[kv]
            if not has_past:
                lg = jnp.where(valid, lg, -1e30)
            logits.append(lg)
        probs = []
        for kv in range(n_kv):
            sink = sink_ref[kv]
            m = jnp.maximum(jnp.max(logits[kv], axis=-1, keepdims=True), sink)
            p = jnp.exp(logits[kv] - m)
            den = jnp.sum(p, axis=-1, keepdims=True) + jnp.exp(sink - m)
            probs.append((p * (1.0 / den)).astype(BF16))
        for kv in range(n_kv):
            v0, v1 = vx[0, kv, krows, :], vx[1, kv, krows, :]
            for half in range(SWA_GROUP // 2):
                g0 = 2 * half * CHUNK
                out = (jnp.dot(probs[kv][g0:g0 + CHUNK], v0, preferred_element_type=F32)
                       + jnp.dot(probs[kv][g0 + CHUNK:g0 + 2 * CHUNK], v1, preferred_element_type=F32))
                c0 = (kv * SWA_GROUP + 2 * half) * head_dim
                o_ref[qrows, c0:c0 + pair] = out.astype(o_ref.dtype)
        return 0

    lax.fori_loop(0, n_chunks, chunk, 0, unroll=2)


def _swa(h, k_cache, v_cache, sinks, bias, *, n_seq, seq_len, q_col, k_col, v_col, q_width, kv_width, window):
    has_past = k_cache is not None
    rows = min(seq_len, 256)
    has_prev_block = seq_len > rows
    assert rows % CHUNK == 0 and (not has_prev_block or rows % window == 0)
    n_heads = sinks.shape[0]
    head_dim = q_width // n_heads
    n_kv = n_heads // SWA_GROUP
    n_k = window + CHUNK
    blocks_per_seq = seq_len // rows
    qb, kb, vb = q_col // q_width, k_col // kv_width, v_col // kv_width
    assert q_col % q_width == 0 and k_col % kv_width == 0 and v_col % kv_width == 0
    bias_g = bias.reshape(n_kv, SWA_GROUP * CHUNK, n_k)
    sink_g = jnp.repeat(sinks.astype(F32).reshape(n_kv, SWA_GROUP), CHUNK, axis=1)[..., None]

    in_specs = [
        pl.BlockSpec((rows, q_width), lambda b, r: (b * blocks_per_seq + r, qb)),
        pl.BlockSpec((rows, kv_width), lambda b, r: (b * blocks_per_seq + r, kb)),
        pl.BlockSpec((rows, kv_width), lambda b, r: (b * blocks_per_seq + r, vb)),
    ]
    args = [h, h, h]
    if has_prev_block:
        wpb = rows // window
        prev = lambda col: pl.BlockSpec(
            (window, kv_width), lambda b, r: (jnp.maximum((b * blocks_per_seq + r) * wpb - 1, 0), col))
        in_specs += [prev(kb), prev(vb)]
        args += [h, h]
    if has_past:
        cache = pl.BlockSpec((None, window, kv_width), lambda b, r: (b, 0, 0))
        in_specs += [cache, cache]
        args += [k_cache.reshape(n_seq, window, kv_width), v_cache.reshape(n_seq, window, kv_width)]
    in_specs += [_resident(bias_g.shape), _resident(sink_g.shape)]
    args += [bias_g, sink_g]

    return pl.pallas_call(
        functools.partial(_swa_body, rows=rows, window=window, n_kv=n_kv, head_dim=head_dim,
                          has_past=has_past, has_prev_block=has_prev_block),
        grid=(n_seq, blocks_per_seq),
        in_specs=in_specs,
        out_specs=pl.BlockSpec((rows, q_width), lambda b, r: (b * blocks_per_seq + r, 0)),
        out_shape=jax.ShapeDtypeStruct((n_seq * seq_len, q_width), BF16),
        scratch_shapes=[pltpu.VMEM((2, n_kv, window + rows, 2 * head_dim), BF16)] * 2,
        compiler_params=_params(("parallel", "arbitrary")),
        name="swa_attention",
    )(*args)


def _merge(x, y_ssm, attn, h, w_glu, w_o, w_out, *, gs_col, ga_col, tm):
    m, d = x.shape
    gcol = math.gcd(gs_col, ga_col, d)
    assert gcol % LANE == 0
    nsub = d // gcol

    def body(x_ref, y_ref, a_ref, *rest):
        gs = jnp.concatenate([r[...] for r in rest[:nsub]], axis=1)
        ga = jnp.concatenate([r[...] for r in rest[nsub:2 * nsub]], axis=1)
        wglu_ref, wo_ref, wout_ref, o_ref = rest[2 * nsub:]
        gl = jnp.dot(y_ref[...], wglu_ref[...], preferred_element_type=F32)
        ys = gl[:, :d] * jax.nn.sigmoid(gl[:, d:])
        ya = jnp.dot(a_ref[...], wo_ref[...], preferred_element_type=F32)
        merged = jax.nn.sigmoid(gs) * ys + jax.nn.sigmoid(ga) * ya
        o_ref[...] = x_ref[...] + jnp.dot(merged.astype(BF16), wout_ref[...], preferred_element_type=F32)

    gate_specs = [pl.BlockSpec((tm, gcol), lambda i, c=c: (i, c))
                  for base in (gs_col, ga_col) for c in range(base // gcol, base // gcol + nsub)]
    return pl.pallas_call(
        body,
        grid=(m // tm,),
        in_specs=[
            pl.BlockSpec((tm, d), lambda i: (i, 0)),
            pl.BlockSpec((tm, y_ssm.shape[1]), lambda i: (i, 0)),
            pl.BlockSpec((tm, attn.shape[1]), lambda i: (i, 0)),
            *gate_specs,
            _resident(w_glu.shape),
            _resident(w_o.shape),
            _resident(w_out.shape),
        ],
        out_specs=pl.BlockSpec((tm, d), lambda i: (i, 0)),
        out_shape=jax.ShapeDtypeStruct((m, d), F32),
        compiler_params=_params(("parallel",)),
        name="merge_out_proj",
    )(x, y_ssm, attn, *([h] * (2 * nsub)), w_glu, w_o, w_out)


def _xattn_body(x_ref, g_ref, mk_ref, mv_ref, wq_ref, wo_ref, o_ref, *, n_heads, head_dim):
    x = x_ref[...]
    q = jnp.dot(_rms(x, g_ref[...]).astype(BF16), wq_ref[...], preferred_element_type=F32).astype(BF16)
    scale = 1.0 / math.sqrt(head_dim)
    outs = []
    for hd in range(n_heads):
        hs = slice(hd * head_dim, (hd + 1) * head_dim)
        logits = lax.dot_general(q[:, hs], mk_ref[:, hs].astype(BF16), (((1,), (1,)), ((), ())),
                                 preferred_element_type=F32) * scale
        m = jnp.max(logits, axis=-1, keepdims=True)
        p = jnp.exp(logits - m)
        probs = p / jnp.sum(p, axis=-1, keepdims=True)
        outs.append(jnp.dot(probs.astype(BF16), mv_ref[:, hs].astype(BF16), preferred_element_type=F32))
    att = jnp.concatenate(outs, axis=1).astype(BF16)
    o_ref[...] = x + jnp.dot(att, wo_ref[...], preferred_element_type=F32)


def _xattn(x, g, mk, mv, w_q, w_o, *, n_seq, seq_len, n_heads):
    m, d = x.shape
    tm = min(seq_len, 256)
    blocks_per_seq = seq_len // tm
    mem_len, xw = mk.shape[1], mk.shape[2]
    return pl.pallas_call(
        functools.partial(_xattn_body, n_heads=n_heads, head_dim=xw // n_heads),
        grid=(n_seq, blocks_per_seq),
        in_specs=[
            pl.BlockSpec((tm, d), lambda b, r: (b * blocks_per_seq + r, 0)),
            _resident((1, d)),
            pl.BlockSpec((None, mem_len, xw), lambda b, r: (b, 0, 0)),
            pl.BlockSpec((None, mem_len, xw), lambda b, r: (b, 0, 0)),
            _resident(w_q.shape),
            _resident(w_o.shape),
        ],
        out_specs=pl.BlockSpec((tm, d), lambda b, r: (b * blocks_per_seq + r, 0)),
        out_shape=jax.ShapeDtypeStruct((m, d), F32),
        compiler_params=_params(("parallel", "arbitrary")),
        name="cross_attention",
    )(x, g.reshape(1, d), mk, mv, w_q, w_o)


def _router_body(xp_ref, xs_ref, g_ref, wr_ref, br_ref, xn_ref, idx_ref, gate_ref, rank_ref, cnt_ref, carry,
                 *, n_exp, n_prompt_tiles):
    i = pl.program_id(0)

    @pl.when(i == 0)
    def _():
        carry[...] = jnp.zeros_like(carry)

    xn = _rms(jnp.where(i < n_prompt_tiles, xp_ref[...], xs_ref[...]), g_ref[...])
    xn_ref[...] = xn
    tm = xn.shape[0]
    logits = lax.dot_general(wr_ref[...], xn, (((1,), (1,)), ((), ())),
                             precision=lax.Precision.HIGHEST, preferred_element_type=F32) + br_ref[...]
    eid = lax.broadcasted_iota(jnp.int32, (n_exp, tm), 0)
    vals, hots = [], []
    work = logits
    for k in range(TOP_K):
        m = jnp.max(work, axis=0, keepdims=True)
        sel = jnp.min(jnp.where(work == m, eid, n_exp), axis=0, keepdims=True)
        hot = eid == sel
        vals.append(m)
        hots.append(hot)
        idx_ref[k:k + 1, :] = sel
        work = jnp.where(hot, -jnp.inf, work)
    ex = [jnp.exp(v - vals[0]) for v in vals]
    tot = ex[0] + ex[1] + ex[2] + ex[3]
    for k in range(TOP_K):
        gate_ref[k:k + 1, :] = ex[k] / tot

    chosen = (hots[0] | hots[1] | hots[2] | hots[3])
    a = jnp.where(chosen, 1.0, 0.0)
    earlier = (lax.broadcasted_iota(jnp.int32, (tm, tm), 0) < lax.broadcasted_iota(jnp.int32, (tm, tm), 1))
    prefix = jnp.dot(a.astype(BF16), jnp.where(earlier, 1.0, 0.0).astype(BF16),
                     preferred_element_type=F32) + carry[...]
    for k in range(TOP_K):
        rank_ref[k:k + 1, :] = jnp.sum(jnp.where(hots[k], prefix, 0.0), axis=0, keepdims=True).astype(jnp.int32)
    carry[...] = carry[...] + jnp.sum(a, axis=1, keepdims=True)
    cnt_ref[...] = carry[...].astype(jnp.int32)


def _two_group_specs(tm, d, npt):
    return [pl.BlockSpec((tm, d), lambda i: (jnp.minimum(i, npt - 1), 0)),
            pl.BlockSpec((tm, d), lambda i: (jnp.maximum(i - npt, 0), 0))]


def _router(x_p, x_s, g, w_router, b_router):
    d = x_p.shape[1]
    t = x_p.shape[0] + x_s.shape[0]
    n_exp = w_router.shape[1]
    tm = ROUTER_TM
    assert x_p.shape[0] % tm == 0 and x_s.shape[0] % tm == 0
    npt = x_p.shape[0] // tm
    row = pl.BlockSpec((TOP_K, tm), lambda i: (0, i))
    return pl.pallas_call(
        functools.partial(_router_body, n_exp=n_exp, n_prompt_tiles=npt),
        grid=(t // tm,),
        in_specs=[
            *_two_group_specs(tm, d, npt),
            _resident((1, d)),
            _resident((n_exp, d)),
            _resident((n_exp, 1)),
        ],
        out_specs=[pl.BlockSpec((tm, d), lambda i: (i, 0)), row, row, row,
                   pl.BlockSpec((n_exp, 1), lambda i: (0, 0))],
        out_shape=[
            jax.ShapeDtypeStruct((t, d), F32),
            jax.ShapeDtypeStruct((TOP_K, t), jnp.int32),
            jax.ShapeDtypeStruct((TOP_K, t), F32),
            jax.ShapeDtypeStruct((TOP_K, t), jnp.int32),
            jax.ShapeDtypeStruct((n_exp, 1), jnp.int32),
        ],
        scratch_shapes=[pltpu.VMEM((n_exp, 1), F32)],
        compiler_params=_params(("arbitrary",)),
        name="moe_router",
    )(x_p, x_s, g.reshape(1, d), w_router.astype(F32).T, b_router.astype(F32).reshape(n_exp, 1))


def _dispatch_body(pos_ref, fill_ref, x_ref, xs_ref, zeros, sem, fill_sem, *, n_exp):
    tm, d = x_ref.shape
    fill_rows = zeros.shape[0]

    def fill_copy(e):
        start = pl.multiple_of(fill_ref[e], SUBLANE)
        return pltpu.make_async_copy(zeros, xs_ref.at[pl.ds(start, fill_rows), :], fill_sem)

    @pl.when(pl.program_id(0) == 0)
    def _():
        zeros[...] = jnp.zeros_like(zeros)
        for e in range(n_exp):
            fill_copy(e).start()
        for e in range(n_exp):
            fill_copy(e).wait()

    def issue(t, _):
        for k in range(TOP_K):
            pltpu.make_async_copy(x_ref.at[pl.ds(t, 1), :], xs_ref.at[pl.ds(pos_ref[k, t], 1), :], sem).start()
        return 0

    lax.fori_loop(0, tm, issue, 0, unroll=8)
    for k in range(TOP_K):
        pltpu.make_async_copy(x_ref, xs_ref.at[pl.ds(0, tm), :], sem).wait()


def _dispatch(xn, pos, fill_start, n_rows):
    t, d = xn.shape
    tm = DISPATCH_TM
    n_exp = fill_start.shape[0]
    fill_rows = MOE_SB + SUBLANE
    return pl.pallas_call(
        functools.partial(_dispatch_body, n_exp=n_exp),
        grid=(t // tm,),
        in_specs=[
            pl.BlockSpec((TOP_K, tm), lambda i: (0, i), memory_space=pltpu.SMEM),
            pl.BlockSpec(memory_space=pltpu.SMEM),
            pl.BlockSpec((tm, d), lambda i: (i, 0)),
        ],
        out_specs=pl.BlockSpec(memory_space=pl.ANY),
        out_shape=jax.ShapeDtypeStruct((n_rows + fill_rows, d), xn.dtype),
        scratch_shapes=[pltpu.VMEM((fill_rows, d), xn.dtype), pltpu.SemaphoreType.DMA, pltpu.SemaphoreType.DMA],
        compiler_params=_params(("arbitrary",)),
        name="moe_dispatch",
    )(pos, fill_start, xn)


def _experts_body(te_ref, row0_ref, nv_ref, xs_ref, wg_ref, wl_ref, bg_ref, bl_ref, wd_ref, bd_ref,
                  ys_ref, acc, xb, stage, wgb, wlb, wdb, in_sem, out_sem, *, n_j, balanced_sub):
    del te_ref
    i, j = pl.program_id(0), pl.program_id(1)
    n_valid = nv_ref[i]
    row0 = row0_ref[i]
    d = acc.shape[1]
    n_sub = (n_valid + MOE_SB - 1) // MOE_SB
    n_big = n_valid // MOE_BIG
    small0 = n_big * MOE_BIG
    n_small = n_sub - n_big * (MOE_BIG // MOE_SB)

    def in_copy(s, slot):
        src = pl.ds(pl.multiple_of(row0 + s * MOE_SB, MOE_SB), MOE_SB)
        return pltpu.make_async_copy(xs_ref.at[src, :], stage.at[slot], in_sem.at[slot])

    def out_copy(first, size):
        src = pl.ds(pl.multiple_of(first, MOE_SB), size)
        dst = pl.ds(pl.multiple_of(row0 + first, MOE_SB), size)
        return pltpu.make_async_copy(acc.at[src, :], ys_ref.at[dst, :], out_sem)

    def drain(blocks):
        def one(s, _):
            out_copy(0, MOE_SB).wait()
            return 0

        lax.fori_loop(0, blocks, one, 0)

    n_tiles = pl.num_programs(0)
    has_next = (i + 1 < n_tiles) & (nv_ref[jnp.minimum(i + 1, n_tiles - 1)] > 0)
    prev_sub = (nv_ref[jnp.maximum(i - 1, 0)] + MOE_SB - 1) // MOE_SB

    def compute(first, size):
        rows = pl.ds(pl.multiple_of(first, MOE_SB), size)
        xs = xb[rows, :]
        hg = jnp.dot(xs, wgb[...], preferred_element_type=F32) + bg_ref[...]
        hl = jnp.dot(xs, wlb[...], preferred_element_type=F32) + bl_ref[...]
        glu = jnp.minimum(hg, SWIGLU_LIMIT)
        lin = jnp.clip(hl, -SWIGLU_LIMIT, SWIGLU_LIMIT)
        act = (glu * jax.nn.sigmoid(SWIGLU_ALPHA * glu) * (lin + 1.0)).astype(BF16)
        for c in range(0, d, MOE_DOWN_COLS):
            cols = slice(c, c + MOE_DOWN_COLS)
            acc[rows, cols] += jnp.dot(act, wdb[:, cols], preferred_element_type=F32)

        @pl.when(j == n_j - 1)
        def _():
            out_copy(first, size).start()

    @pl.when(n_valid > 0)
    def _():
        @pl.when(j == 0)
        def _():
            in_copy(0, 0).start()

            @pl.when(i > 0)
            def _():
                drain(prev_sub)

            def load(s, _):
                slot = s % 2
                rows = pl.ds(pl.multiple_of(s * MOE_SB, MOE_SB), MOE_SB)

                @pl.when(s + 1 < n_sub)
                def _():
                    in_copy(s + 1, 1 - slot).start()

                in_copy(s, slot).wait()
                xb[rows, :] = stage[slot].astype(BF16)
                acc[rows, :] = jnp.broadcast_to(bd_ref[...], (MOE_SB, d))
                return 0

            lax.fori_loop(0, n_sub, load, 0)

        wgb[...] = wg_ref[...].astype(BF16)
        wlb[...] = wl_ref[...].astype(BF16)
        wdb[...] = wd_ref[...].astype(BF16)

        def big(s, _):
            compute(s * MOE_BIG, MOE_BIG)
            return 0

        def small(s, _):
            compute(small0 + s * MOE_SB, MOE_SB)
            return 0

        balanced = n_sub == balanced_sub

        @pl.when(balanced)
        def _():
            compute(0, balanced_sub * MOE_SB)

        @pl.when(jnp.logical_not(balanced))
        def _():
            lax.fori_loop(0, n_big, big, 0)
            lax.fori_loop(0, n_small, small, 0)

        @pl.when((j == n_j - 1) & jnp.logical_not(has_next))
        def _():
            drain(n_sub)


def _experts(xs, tile_expert, tile_row0, tile_valid, n_rows, n_assign, w_up, b_up, w_down, b_down):
    d = xs.shape[1]
    n_tiles = tile_expert.shape[0]
    depth, n_exp, _, ff2 = w_up.shape
    assert depth == 1
    balanced_sub = min(-(-n_assign // (n_exp * MOE_SB)), MOE_CAP // MOE_SB)
    d_ff = ff2 // 2
    n_j = d_ff // MOE_TF
    b_up = b_up.reshape(depth, n_exp, 1, ff2)
    b_down = b_down.reshape(depth, n_exp, 1, d)

    def jsel(i, j, nv):
        return jnp.where(nv[i] > 0, j, n_j - 1)

    grid_spec = pltpu.PrefetchScalarGridSpec(
        num_scalar_prefetch=3,
        grid=(n_tiles, n_j),
        in_specs=[
            pl.BlockSpec(memory_space=pl.ANY),
            pl.BlockSpec((None, None, d, MOE_TF), lambda i, j, te, r0, nv: (0, te[i], 0, jsel(i, j, nv))),
            pl.BlockSpec((None, None, d, MOE_TF), lambda i, j, te, r0, nv: (0, te[i], 0, n_j + jsel(i, j, nv))),
            pl.BlockSpec((None, None, 1, MOE_TF), lambda i, j, te, r0, nv: (0, te[i], 0, jsel(i, j, nv))),
            pl.BlockSpec((None, None, 1, MOE_TF), lambda i, j, te, r0, nv: (0, te[i], 0, n_j + jsel(i, j, nv))),
            pl.BlockSpec((None, None, MOE_TF, d), lambda i, j, te, r0, nv: (0, te[i], jsel(i, j, nv), 0)),
            pl.BlockSpec((None, None, 1, d), lambda i, j, te, r0, nv: (0, te[i], 0, 0)),
        ],
        out_specs=pl.BlockSpec(memory_space=pl.ANY),
        scratch_shapes=[
            pltpu.VMEM((MOE_CAP, d), F32),
            pltpu.VMEM((MOE_CAP, d), BF16),
            pltpu.VMEM((2, MOE_SB, d), F32),
            pltpu.VMEM((d, MOE_TF), BF16),
            pltpu.VMEM((d, MOE_TF), BF16),
            pltpu.VMEM((MOE_TF, d), BF16),
            pltpu.SemaphoreType.DMA((2,)),
            pltpu.SemaphoreType.DMA,
        ],
    )
    return pl.pallas_call(
        functools.partial(_experts_body, n_j=n_j, balanced_sub=balanced_sub),
        grid_spec=grid_spec,
        out_shape=jax.ShapeDtypeStruct((n_rows, d), F32),
        compiler_params=_params(("arbitrary", "arbitrary")),
        name="moe_experts",
    )(tile_expert, tile_row0, tile_valid, xs, w_up, w_up, b_up, b_up, w_down, b_down)


def _combine_body(pos_ref, nxt_ref, xp_ref, xs_ref, gate_ref, g_ref, ys_ref, op_ref, os_ref, buf, sem,
                  *, n_prompt_tiles):
    i = pl.program_id(0)
    n = pl.num_programs(0)
    tm = xp_ref.shape[0]

    def gather(p_ref, slot):
        def issue(t, _):
            for k in range(TOP_K):
                pltpu.make_async_copy(ys_ref.at[pl.ds(p_ref[k, t], 1), :], buf.at[slot, k, pl.ds(t, 1), :],
                                      sem.at[slot]).start()
            return 0

        lax.fori_loop(0, tm, issue, 0, unroll=8)

    slot = i % 2

    @pl.when(i == 0)
    def _():
        gather(pos_ref, 0)

    @pl.when(i + 1 < n)
    def _():
        gather(nxt_ref, 1 - slot)

    for k in range(TOP_K):
        pltpu.make_async_copy(ys_ref.at[pl.ds(0, tm), :], buf.at[slot, k], sem.at[slot]).wait()

    out = jnp.where(i < n_prompt_tiles, xp_ref[...], xs_ref[...])
    for k in range(TOP_K):
        out = out + gate_ref[:, k:k + 1] * buf[slot, k]
    y = _rms(out, g_ref[...])

    @pl.when(i < n_prompt_tiles)
    def _():
        op_ref[...] = y

    @pl.when(i >= n_prompt_tiles)
    def _():
        os_ref[...] = y


def _combine(x_p, x_s, ys, pos, gate_t, g_final):
    n_prompt, d = x_p.shape
    t = n_prompt + x_s.shape[0]
    tm = TOK_TM
    assert n_prompt % tm == 0 and t % tm == 0
    npt = n_prompt // tm
    n = t // tm
    return pl.pallas_call(
        functools.partial(_combine_body, n_prompt_tiles=npt),
        grid=(n,),
        in_specs=[
            pl.BlockSpec((TOP_K, tm), lambda i: (0, i), memory_space=pltpu.SMEM),
            pl.BlockSpec((TOP_K, tm), lambda i: (0, jnp.minimum(i + 1, n - 1)), memory_space=pltpu.SMEM),
            *_two_group_specs(tm, d, npt),
            pl.BlockSpec((tm, TOP_K), lambda i: (i, 0)),
            _resident((1, d)),
            pl.BlockSpec(memory_space=pl.ANY),
        ],
        out_specs=_two_group_specs(tm, d, npt),
        out_shape=[
            jax.ShapeDtypeStruct((n_prompt, d), F32),
            jax.ShapeDtypeStruct((t - n_prompt, d), F32),
        ],
        scratch_shapes=[pltpu.VMEM((2, TOP_K, tm, d), F32), pltpu.SemaphoreType.DMA((2,))],
        compiler_params=_params(("arbitrary",)),
        name="moe_combine",
    )(pos, pos, x_p, x_s, gate_t, g_final.reshape(1, d), ys)


def _moe_layout(idx, rank, counts, n_tiles):
    n_exp = counts.shape[0]
    region = ((counts + MOE_SB - 1) // MOE_SB) * MOE_SB
    row_start = jnp.cumsum(region) - region
    tiles_per_exp = (counts + MOE_CAP - 1) // MOE_CAP
    tile_end = jnp.cumsum(tiles_per_exp)
    tile_start = tile_end - tiles_per_exp
    n_used = tile_end[-1]
    tiles = jnp.arange(n_tiles, dtype=jnp.int32)
    te = jnp.minimum(jnp.sum(tiles[:, None] >= tile_end[None, :], axis=1), n_exp - 1).astype(jnp.int32)
    te = jnp.where(tiles < n_used, te, te[jnp.maximum(n_used - 1, 0)])
    first = (tiles - tile_start[te]) * MOE_CAP
    valid = jnp.where(tiles < n_used, jnp.clip(counts[te] - first, 0, MOE_CAP), 0)
    row0 = jnp.where(tiles < n_used, row_start[te] + first, 0)
    hot = idx[..., None] == jnp.arange(n_exp, dtype=jnp.int32)
    pos = rank + jnp.sum(jnp.where(hot, row_start, 0), axis=-1)
    fill_start = row_start + (counts // SUBLANE) * SUBLANE
    i32 = lambda a: a.astype(jnp.int32)
    return i32(pos), te, i32(row0), i32(valid), i32(fill_start)


def kernel(x_prompt, x_sample, state_ssm_re, state_ssm_im, cache_swa_k, cache_swa_v, cache_mem_k, cache_mem_v, mem_prompt, w_in, ssm_lam_re, ssm_lam_im, ssm_log_dt, ssm_b_re, ssm_b_im, ssm_c_re, ssm_c_im, ssm_d, w_glu, swa_sinks, rel_bias, w_o_attn, w_out, w_xq, w_mk, w_mv, w_xo, w_router, b_router, w_up, b_up, w_down, b_down, norm_mix, norm_x, norm_mem, norm_moe, norm_final):
    depth = w_in.shape[0]
    assert depth == 1, "single-layer step"
    bp, lp, d = x_prompt.shape
    bs, ls, _ = x_sample.shape
    n_p, n_s = bp * lp, bs * ls
    t = n_p + n_s
    _, _, window, kvh, hd = cache_swa_k.shape
    kv_width = kvh * hd
    n_heads = swa_sinks.shape[1]
    q_width = n_heads * hd
    ssm_width = ssm_d.shape[1]
    _, _, mem_len, x_heads, x_hd = cache_mem_k.shape
    xw = x_heads * x_hd
    ssm_groups, ssm_state = ssm_lam_re.shape[1], ssm_lam_re.shape[2]
    q_col = ssm_width
    k_col = q_col + q_width
    v_col = k_col + kv_width
    gs_col = v_col + kv_width
    ga_col = gs_col + d

    l = 0
    w_in_b = w_in[l].astype(BF16)
    w_glu_b, w_o_b, w_out_b = w_glu[l].astype(BF16), w_o_attn[l].astype(BF16), w_out[l].astype(BF16)
    w_xq_b, w_xo_b = w_xq[l].astype(BF16), w_xo[l].astype(BF16)
    w_mkv_b = jnp.concatenate([w_mk[l], w_mv[l]], axis=1).astype(BF16)
    a_re, a_im, bcat, ccat = _s5_discretize(ssm_lam_re[l], ssm_lam_im[l], ssm_log_dt[l], ssm_b_re[l],
                                            ssm_b_im[l], ssm_c_re[l], ssm_c_im[l])
    bucket = _t5_bucket_table(CHUNK, window + CHUNK, window)
    bias = jnp.transpose(rel_bias.astype(F32)[bucket], (2, 0, 1))

    def mixers(x, n_seq, seq_len, s0_re, s0_im, k_cache, v_cache):
        xf = x.reshape(n_seq * seq_len, d)
        h = _norm_matmul(xf, norm_mix[l], w_in_b, tm=1024, tn=1664)
        y_ssm, s_re, s_im = _s5(h[:, :ssm_width], s0_re, s0_im, a_re, a_im, bcat, ccat, ssm_d[l],
                                n_seq=n_seq, seq_len=seq_len)
        attn = _swa(h, k_cache, v_cache, swa_sinks[l], bias, n_seq=n_seq, seq_len=seq_len, q_col=q_col,
                    k_col=k_col, v_col=v_col, q_width=q_width, kv_width=kv_width, window=window)
        x1 = _merge(xf, y_ssm, attn, h, w_glu_b, w_o_b, w_out_b, gs_col=gs_col, ga_col=ga_col, tm=256)
        h3 = h.reshape(n_seq, seq_len, -1)
        k_new = h3[:, :, k_col:k_col + kv_width]
        v_new = h3[:, :, v_col:v_col + kv_width]
        if k_cache is not None:
            k_new = jnp.concatenate([k_cache.reshape(n_seq, window, kv_width), k_new], axis=1)
            v_new = jnp.concatenate([v_cache.reshape(n_seq, window, kv_width), v_new], axis=1)
        k_new = k_new[:, -window:].reshape(1, n_seq, window, kvh, hd)
        v_new = v_new[:, -window:].reshape(1, n_seq, window, kvh, hd)
        state_shape = (1, n_seq, ssm_groups, ssm_state)
        return x1, s_re.reshape(state_shape), s_im.reshape(state_shape), k_new, v_new

    zero_state = jnp.zeros((bp, ssm_groups * ssm_state), F32)
    x1_p, sre_p, sim_p, k_p, v_p = mixers(x_prompt, bp, lp, zero_state, zero_state, None, None)
    x1_s, sre_s, sim_s, k_s, v_s = mixers(x_sample, bs, ls, state_ssm_re[l], state_ssm_im[l],
                                          cache_swa_k[l], cache_swa_v[l])

    mkv = _norm_matmul(mem_prompt.reshape(bp * mem_len, d), norm_mem[l], w_mkv_b, tm=1024, tn=512)
    mkv = mkv.reshape(bp, mem_len, 2 * xw)
    mk_p, mv_p = mkv[:, :, :xw], mkv[:, :, xw:]

    x2_p = _xattn(x1_p, norm_x[l], mk_p, mv_p, w_xq_b, w_xo_b, n_seq=bp, seq_len=lp, n_heads=x_heads)
    x2_s = _xattn(x1_s, norm_x[l], cache_mem_k[l].reshape(bs, mem_len, xw), cache_mem_v[l].reshape(bs, mem_len, xw),
                  w_xq_b, w_xo_b, n_seq=bs, seq_len=ls, n_heads=x_heads)

    n_exp = w_router.shape[2]
    xn, idx, gate, rank, counts = _router(x2_p, x2_s, norm_moe[l], w_router[l], b_router[l])
    n_tiles = (t * TOP_K) // MOE_CAP + n_exp
    n_rows = t * TOP_K + n_exp * MOE_SB
    pos, tile_expert, tile_row0, tile_valid, fill_start = _moe_layout(idx, rank, counts[:, 0], n_tiles)
    xs = _dispatch(xn, pos, fill_start, n_rows)
    ys = _experts(xs, tile_expert, tile_row0, tile_valid, n_rows, t * TOP_K, w_up, b_up, w_down, b_down)
    y_p, y_s = _combine(x2_p, x2_s, ys, pos, gate.T, norm_final)

    return (y_p.reshape(bp, lp, d), y_s.reshape(bs, ls, d), sre_p, sim_p, k_p, v_p,
            mk_p.reshape(1, bp, mem_len, x_heads, x_hd), mv_p.reshape(1, bp, mem_len, x_heads, x_hd),
            sre_s, sim_s, k_s, v_s)
```

```python
import functools
import math

import numpy as np
import jax
import jax.numpy as jnp
from jax import lax
from jax.experimental import pallas as pl
from jax.experimental.pallas import tpu as pltpu

F32 = jnp.float32
BF16 = jnp.bfloat16

LANE = 128
SUBLANE = 8
VMEM_LIMIT = 56 * 1024 * 1024

CHUNK = 64
SSM_GROUP = 16
SWA_GROUP = 4
REL_BUCKETS = 32
REL_MAX_DIST = 128
TOP_K = 4
SWIGLU_LIMIT = 7.0
SWIGLU_ALPHA = 1.702
RMS_EPS = 1e-5
SEQ_PER_SCAN = SUBLANE

S5_STRIP = 512
MOE_CAP = 2560
MOE_SB = 256
MOE_BIG = 1024
MOE_DOWN_COLS = 512
MOE_TF = 256
TOK_TM = 256
DISPATCH_TM = 512
ROUTER_TM = 512


def _params(sem, vmem=VMEM_LIMIT):
    return pltpu.CompilerParams(dimension_semantics=sem, vmem_limit_bytes=vmem)


def _rms(x, g):
    return x * lax.rsqrt(jnp.mean(x * x, axis=-1, keepdims=True) + RMS_EPS) * g


def _resident(shape):
    nd = len(shape)
    return pl.BlockSpec(shape, lambda *_: (0,) * nd, pipeline_mode=pl.Buffered(1))


def _norm_matmul_body(x_ref, g_ref, w_ref, o_ref, xn_ref):
    @pl.when(pl.program_id(1) == 0)
    def _():
        xn_ref[...] = _rms(x_ref[...], g_ref[...]).astype(BF16)

    o_ref[...] = jnp.dot(xn_ref[...], w_ref[...], preferred_element_type=F32)


def _norm_matmul(x, g, w, *, tm, tn):
    m, d = x.shape
    n = w.shape[1]
    tm = min(tm, m)
    assert m % tm == 0 and n % tn == 0
    return pl.pallas_call(
        _norm_matmul_body,
        grid=(m // tm, n // tn),
        in_specs=[
            pl.BlockSpec((tm, d), lambda i, j: (i, 0)),
            pl.BlockSpec((1, d), lambda i, j: (0, 0)),
            pl.BlockSpec((d, tn), lambda i, j: (0, j)),
        ],
        out_specs=pl.BlockSpec((tm, tn), lambda i, j: (i, j)),
        out_shape=jax.ShapeDtypeStruct((m, n), F32),
        scratch_shapes=[pltpu.VMEM((tm, d), BF16)],
        compiler_params=_params(("parallel", "arbitrary")),
        name="norm_matmul",
    )(x, g.reshape(1, d), w)


def _s5_discretize(lam_re, lam_im, log_dt, b_re, b_im, c_re, c_im):
    g, p = lam_re.shape
    gpb = LANE // SSM_GROUP
    nblk = g // gpb
    lr, li = lam_re.astype(F32), lam_im.astype(F32)
    dt = jnp.exp(log_dt.astype(F32))[:, None]
    mag = jnp.exp(lr * dt)
    a_re = mag * jnp.cos(li * dt)
    a_im = mag * jnp.sin(li * dt)
    den = lr * lr + li * li
    f_re = ((a_re - 1.0) * lr + a_im * li) / den
    f_im = (a_im * lr - (a_re - 1.0) * li) / den
    br, bi = b_re.astype(F32), b_im.astype(F32)
    bb_re = f_re[..., None] * br - f_im[..., None] * bi
    bb_im = f_re[..., None] * bi + f_im[..., None] * br
    eye = jnp.eye(gpb, dtype=F32)

    def b_blocks(bb):
        t = jnp.transpose(bb, (0, 2, 1)).reshape(nblk, gpb, SSM_GROUP, p)
        return jnp.einsum("jghp,gk->jghkp", t, eye).reshape(nblk, gpb * SSM_GROUP, gpb * p)

    def c_blocks(c):
        t = c.astype(F32).reshape(nblk, gpb, SSM_GROUP, p)
        return jnp.einsum("jghp,gk->jgpkh", t, eye).reshape(nblk, gpb * p, gpb * SSM_GROUP)

    bcat = jnp.concatenate([b_blocks(bb_re), b_blocks(bb_im)], axis=2).astype(BF16)
    ccat = jnp.concatenate([c_blocks(c_re), -c_blocks(c_im)], axis=1).astype(BF16)
    bro = lambda a: jnp.broadcast_to(a.reshape(1, g * p), (SUBLANE, g * p))
    return bro(a_re), bro(a_im), bcat, ccat


def _s5_body(u_ref, s0re_ref, s0im_ref, are_ref, aim_ref, bcat_ref, ccat_ref, d_ref,
             y_ref, ore_ref, oim_ref, sre, sim, st_re, st_im, *, tc, nblk, blk_state):
    c = pl.program_id(1)

    @pl.when(c == 0)
    def _():
        st_re[...] = s0re_ref[...]
        st_im[...] = s0im_ref[...]

    for j in range(nblk):
        ub = u_ref[:, j * LANE:(j + 1) * LANE].astype(BF16)
        bu = jnp.dot(ub, bcat_ref[j], preferred_element_type=F32)
        sre[:, j * blk_state:(j + 1) * blk_state] = bu[:, :blk_state]
        sim[:, j * blk_state:(j + 1) * blk_state] = bu[:, blk_state:]

    n_state = sre.shape[1]
    for strip in range(n_state // S5_STRIP):
        sl = slice(strip * S5_STRIP, (strip + 1) * S5_STRIP)
        ar = are_ref[:, sl]
        ai = aim_ref[:, sl]

        def step(t, carry, sl=sl, ar=ar, ai=ai):
            sr, si = carry
            rows = pl.ds(pl.multiple_of(t * SUBLANE, SUBLANE), SUBLANE)
            nr = ar * sr - ai * si + sre[rows, sl]
            ni = ar * si + ai * sr + sim[rows, sl]
            sre[rows, sl] = nr
            sim[rows, sl] = ni
            return nr, ni

        sr, si = lax.fori_loop(0, tc, step, (st_re[:, sl], st_im[:, sl]), unroll=4)
        st_re[:, sl] = sr
        st_im[:, sl] = si

    for j in range(nblk):
        cols = slice(j * blk_state, (j + 1) * blk_state)
        scat = jnp.concatenate([sre[:, cols].astype(BF16), sim[:, cols].astype(BF16)], axis=1)
        yj = jnp.dot(scat, ccat_ref[j], preferred_element_type=F32)
        lanes = slice(j * LANE, (j + 1) * LANE)
        yj = yj + d_ref[:, lanes] * u_ref[:, lanes]
        y_ref[:, lanes] = jax.nn.gelu(yj).astype(y_ref.dtype)

    @pl.when(c == pl.num_programs(1) - 1)
    def _():
        ore_ref[...] = st_re[...]
        oim_ref[...] = st_im[...]


def _s5(u, s0_re, s0_im, a_re, a_im, bcat, ccat, d_skip, *, n_seq, seq_len):
    w = u.shape[1]
    n_state = a_re.shape[1]
    nblk = bcat.shape[0]
    ng = n_seq // SEQ_PER_SCAN
    tc = min(seq_len, 32)
    rows = tc * SEQ_PER_SCAN
    u_tb = u.reshape(ng, SEQ_PER_SCAN, seq_len, w).transpose(0, 2, 1, 3).reshape(ng, seq_len * SEQ_PER_SCAN, w)
    s0_re = s0_re.astype(F32).reshape(ng, SEQ_PER_SCAN, n_state)
    s0_im = s0_im.astype(F32).reshape(ng, SEQ_PER_SCAN, n_state)
    state_spec = pl.BlockSpec((None, SEQ_PER_SCAN, n_state), lambda g, c: (g, 0, 0))
    y_tb, o_re, o_im = pl.pallas_call(
        functools.partial(_s5_body, tc=tc, nblk=nblk, blk_state=n_state // nblk),
        grid=(ng, seq_len // tc),
        in_specs=[
            pl.BlockSpec((None, rows, w), lambda g, c: (g, c, 0)),
            state_spec,
            state_spec,
            _resident(a_re.shape),
            _resident(a_im.shape),
            _resident(bcat.shape),
            _resident(ccat.shape),
            _resident((1, w)),
        ],
        out_specs=[
            pl.BlockSpec((None, rows, w), lambda g, c: (g, c, 0)),
            state_spec,
            state_spec,
        ],
        out_shape=[
            jax.ShapeDtypeStruct((ng, seq_len * SEQ_PER_SCAN, w), BF16),
            jax.ShapeDtypeStruct((ng, SEQ_PER_SCAN, n_state), F32),
            jax.ShapeDtypeStruct((ng, SEQ_PER_SCAN, n_state), F32),
        ],
        scratch_shapes=[
            pltpu.VMEM((rows, n_state), F32),
            pltpu.VMEM((rows, n_state), F32),
            pltpu.VMEM((SEQ_PER_SCAN, n_state), F32),
            pltpu.VMEM((SEQ_PER_SCAN, n_state), F32),
        ],
        compiler_params=_params(("parallel", "arbitrary")),
        name="s5_scan",
    )(u_tb, s0_re, s0_im, a_re, a_im, bcat, ccat, d_skip.astype(F32).reshape(1, w))
    y = y_tb.reshape(ng, seq_len, SEQ_PER_SCAN, w).transpose(0, 2, 1, 3).reshape(n_seq * seq_len, w)
    return y, o_re.reshape(n_seq, n_state), o_im.reshape(n_seq, n_state)


def _t5_bucket_table(n_q, n_k, n_prev):
    half = REL_BUCKETS // 2
    max_exact = half // 2
    assert (REL_MAX_DIST // max_exact) ** 2 == 2 ** (half - max_exact)
    rel = np.arange(n_k)[None, :] - n_prev - np.arange(n_q)[:, None]
    n = np.abs(rel)
    k = np.zeros_like(n)
    for kk in range(1, 2 * half):
        k = np.where(n * n >= max_exact * max_exact * (2 ** kk), kk, k)
    large = np.minimum(max_exact + k, half - 1)
    return ((rel > 0) * half + np.where(n < max_exact, n, large)).astype(np.int32)


def _swa_body(*refs, rows, window, n_kv, head_dim, has_past, has_prev_block):
    it = iter(refs)
    q_ref, kc_ref, vc_ref = next(it), next(it), next(it)
    kp_ref = vp_ref = kcache_ref = vcache_ref = None
    if has_prev_block:
        kp_ref, vp_ref = next(it), next(it)
    if has_past:
        kcache_ref, vcache_ref = next(it), next(it)
    bias_ref, sink_ref = next(it), next(it)
    o_ref = next(it)
    kx, vx = next(it), next(it)

    r = pl.program_id(1)
    n_k = window + CHUNK
    pair = 2 * head_dim

    def put(dst, lo, src):
        n = src.shape[0]
        zero = jnp.zeros((n, head_dim), F32)
        for kv in range(n_kv):
            x = src[:, kv * head_dim:(kv + 1) * head_dim]
            dst[0, kv, lo:lo + n, :] = jnp.concatenate([x, zero], axis=1).astype(BF16)
            dst[1, kv, lo:lo + n, :] = jnp.concatenate([zero, x], axis=1).astype(BF16)

    @pl.when(r == 0)
    def _():
        if has_past:
            put(kx, 0, kcache_ref[...])
            put(vx, 0, vcache_ref[...])
        else:
            kx[:, :, 0:window, :] = jnp.zeros((2, n_kv, window, pair), BF16)
            vx[:, :, 0:window, :] = jnp.zeros((2, n_kv, window, pair), BF16)

    if has_prev_block:
        @pl.when(r > 0)
        def _():
            put(kx, 0, kp_ref[...])
            put(vx, 0, vp_ref[...])

    put(kx, window, kc_ref[...])
    put(vx, window, vc_ref[...])

    scale = 1.0 / math.sqrt(head_dim)
    n_chunks = rows // CHUNK
    key_pos = lax.broadcasted_iota(jnp.int32, (SWA_GROUP * CHUNK, n_k), 1)
    nt = lambda a, b: lax.dot_general(a, b, (((1,), (1,)), ((), ())), preferred_element_type=F32)

    def chunk(cc, _):
        q0 = pl.multiple_of(cc * CHUNK, CHUNK)
        qrows = pl.ds(q0, CHUNK)
        krows = pl.ds(q0, n_k)
        if not has_past:
            valid = key_pos + (r * n_chunks + cc) * CHUNK >= window
        logits = []
        for kv in range(n_kv):
            k0, k1 = kx[0, kv, krows, :], kx[1, kv, krows, :]
            parts = []
            for half in range(SWA_GROUP // 2):
                c0 = (kv * SWA_GROUP + 2 * half) * head_dim
                qp = (q_ref[qrows, c0:c0 + pair] * scale).astype(BF16)
                parts += [nt(qp, k0), nt(qp, k1)]
            lg = jnp.concatenate(parts, axis=0) + bias_ref[kv]
            if not has_past:
                lg = jnp.where(valid, lg, -1e30)
            logits.append(lg)
        probs = []
        for kv in range(n_kv):
            sink = sink_ref[kv]
            m = jnp.maximum(jnp.max(logits[kv], axis=-1, keepdims=True), sink)
            p = jnp.exp(logits[kv] - m)
            den = jnp.sum(p, axis=-1, keepdims=True) + jnp.exp(sink - m)
            probs.append((p * (1.0 / den)).astype(BF16))
        for kv in range(n_kv):
            v0, v1 = vx[0, kv, krows, :], vx[1, kv, krows, :]
            for half in range(SWA_GROUP // 2):
                g0 = 2 * half * CHUNK
                out = (jnp.dot(probs[kv][g0:g0 + CHUNK], v0, preferred_element_type=F32)
                       + jnp.dot(probs[kv][g0 + CHUNK:g0 + 2 * CHUNK], v1, preferred_element_type=F32))
                c0 = (kv * SWA_GROUP + 2 * half) * head_dim
                o_ref[qrows, c0:c0 + pair] = out.astype(o_ref.dtype)
        return 0

    lax.fori_loop(0, n_chunks, chunk, 0, unroll=2)


def _swa(h, k_cache, v_cache, sinks, bias, *, n_seq, seq_len, q_col, k_col, v_col, q_width, kv_width, window):
    has_past = k_cache is not None
    rows = min(seq_len, 256)
    has_prev_block = seq_len > rows
    assert rows % CHUNK == 0 and (not has_prev_block or rows % window == 0)
    n_heads = sinks.shape[0]
    head_dim = q_width // n_heads
    n_kv = n_heads // SWA_GROUP
    n_k = window + CHUNK
    blocks_per_seq = seq_len // rows
    qb, kb, vb = q_col // q_width, k_col // kv_width, v_col // kv_width
    assert q_col % q_width == 0 and k_col % kv_width == 0 and v_col % kv_width == 0
    bias_g = bias.reshape(n_kv, SWA_GROUP * CHUNK, n_k)
    sink_g = jnp.repeat(sinks.astype(F32).reshape(n_kv, SWA_GROUP), CHUNK, axis=1)[..., None]

    in_specs = [
        pl.BlockSpec((rows, q_width), lambda b, r: (b * blocks_per_seq + r, qb)),
        pl.BlockSpec((rows, kv_width), lambda b, r: (b * blocks_per_seq + r, kb)),
        pl.BlockSpec((rows, kv_width), lambda b, r: (b * blocks_per_seq + r, vb)),
    ]
    args = [h, h, h]
    if has_prev_block:
        wpb = rows // window
        prev = lambda col: pl.BlockSpec(
            (window, kv_width), lambda b, r: (jnp.maximum((b * blocks_per_seq + r) * wpb - 1, 0), col))
        in_specs += [prev(kb), prev(vb)]
        args += [h, h]
    if has_past:
        cache = pl.BlockSpec((None, window, kv_width), lambda b, r: (b, 0, 0))
        in_specs += [cache, cache]
        args += [k_cache.reshape(n_seq, window, kv_width), v_cache.reshape(n_seq, window, kv_width)]
    in_specs += [_resident(bias_g.shape), _resident(sink_g.shape)]
    args += [bias_g, sink_g]

    return pl.pallas_call(
        functools.partial(_swa_body, rows=rows, window=window, n_kv=n_kv, head_dim=head_dim,
                          has_past=has_past, has_prev_block=has_prev_block),
        grid=(n_seq, blocks_per_seq),
        in_specs=in_specs,
        out_specs=pl.BlockSpec((rows, q_width), lambda b, r: (b * blocks_per_seq + r, 0)),
        out_shape=jax.ShapeDtypeStruct((n_seq * seq_len, q_width), BF16),
        scratch_shapes=[pltpu.VMEM((2, n_kv, window + rows, 2 * head_dim), BF16)] * 2,
        compiler_params=_params(("parallel", "arbitrary")),
        name="swa_attention",
    )(*args)


def _merge(x, y_ssm, attn, h, w_glu, w_o, w_out, *, gs_col, ga_col, tm):
    m, d = x.shape
    gcol = math.gcd(gs_col, ga_col, d)
    assert gcol % LANE == 0
    nsub = d // gcol

    def body(x_ref, y_ref, a_ref, *rest):
        gs = jnp.concatenate([r[...] for r in rest[:nsub]], axis=1)
        ga = jnp.concatenate([r[...] for r in rest[nsub:2 * nsub]], axis=1)
        wglu_ref, wo_ref, wout_ref, o_ref = rest[2 * nsub:]
        gl = jnp.dot(y_ref[...], wglu_ref[...], preferred_element_type=F32)
        ys = gl[:, :d] * jax.nn.sigmoid(gl[:, d:])
        ya = jnp.dot(a_ref[...], wo_ref[...], preferred_element_type=F32)
        merged = jax.nn.sigmoid(gs) * ys + jax.nn.sigmoid(ga) * ya
        o_ref[...] = x_ref[...] + jnp.dot(merged.astype(BF16), wout_ref[...], preferred_element_type=F32)

    gate_specs = [pl.BlockSpec((tm, gcol), lambda i, c=c: (i, c))
                  for base in (gs_col, ga_col) for c in range(base // gcol, base // gcol + nsub)]
    return pl.pallas_call(
        body,
        grid=(m // tm,),
        in_specs=[
            pl.BlockSpec((tm, d), lambda i: (i, 0)),
            pl.BlockSpec((tm, y_ssm.shape[1]), lambda i: (i, 0)),
            pl.BlockSpec((tm, attn.shape[1]), lambda i: (i, 0)),
            *gate_specs,
            _resident(w_glu.shape),
            _resident(w_o.shape),
            _resident(w_out.shape),
        ],
        out_specs=pl.BlockSpec((tm, d), lambda i: (i, 0)),
        out_shape=jax.ShapeDtypeStruct((m, d), F32),
        compiler_params=_params(("parallel",)),
        name="merge_out_proj",
    )(x, y_ssm, attn, *([h] * (2 * nsub)), w_glu, w_o, w_out)


def _xattn_body(x_ref, g_ref, mk_ref, mv_ref, wq_ref, wo_ref, o_ref, *, n_heads, head_dim):
    x = x_ref[...]
    q = jnp.dot(_rms(x, g_ref[...]).astype(BF16), wq_ref[...], preferred_element_type=F32).astype(BF16)
    scale = 1.0 / math.sqrt(head_dim)
    outs = []
    for hd in range(n_heads):
        hs = slice(hd * head_dim, (hd + 1) * head_dim)
        logits = lax.dot_general(q[:, hs], mk_ref[:, hs].astype(BF16), (((1,), (1,)), ((), ())),
                                 preferred_element_type=F32) * scale
        m = jnp.max(logits, axis=-1, keepdims=True)
        p = jnp.exp(logits - m)
        probs = p / jnp.sum(p, axis=-1, keepdims=True)
        outs.append(jnp.dot(probs.astype(BF16), mv_ref[:, hs].astype(BF16), preferred_element_type=F32))
    att = jnp.concatenate(outs, axis=1).astype(BF16)
    o_ref[...] = x + jnp.dot(att, wo_ref[...], preferred_element_type=F32)


def _xattn(x, g, mk, mv, w_q, w_o, *, n_seq, seq_len, n_heads):
    m, d = x.shape
    tm = min(seq_len, 256)
    blocks_per_seq = seq_len // tm
    mem_len, xw = mk.shape[1], mk.shape[2]
    return pl.pallas_call(
        functools.partial(_xattn_body, n_heads=n_heads, head_dim=xw // n_heads),
        grid=(n_seq, blocks_per_seq),
        in_specs=[
            pl.BlockSpec((tm, d), lambda b, r: (b * blocks_per_seq + r, 0)),
            _resident((1, d)),
            pl.BlockSpec((None, mem_len, xw), lambda b, r: (b, 0, 0)),
            pl.BlockSpec((None, mem_len, xw), lambda b, r: (b, 0, 0)),
            _resident(w_q.shape),
            _resident(w_o.shape),
        ],
        out_specs=pl.BlockSpec((tm, d), lambda b, r: (b * blocks_per_seq + r, 0)),
        out_shape=jax.ShapeDtypeStruct((m, d), F32),
        compiler_params=_params(("parallel", "arbitrary")),
        name="cross_attention",
    )(x, g.reshape(1, d), mk, mv, w_q, w_o)


def _router_body(xp_ref, xs_ref, g_ref, wr_ref, br_ref, xn_ref, idx_ref, gate_ref, rank_ref, cnt_ref, carry,
                 *, n_exp, n_prompt_tiles):
    i = pl.program_id(0)

    @pl.when(i == 0)
    def _():
        carry[...] = jnp.zeros_like(carry)

    xn = _rms(jnp.where(i < n_prompt_tiles, xp_ref[...], xs_ref[...]), g_ref[...])
    xn_ref[...] = xn
    tm = xn.shape[0]
    logits = lax.dot_general(wr_ref[...], xn, (((1,), (1,)), ((), ())),
                             precision=lax.Precision.HIGHEST, preferred_element_type=F32) + br_ref[...]
    eid = lax.broadcasted_iota(jnp.int32, (n_exp, tm), 0)
    vals, hots = [], []
    work = logits
    for k in range(TOP_K):
        m = jnp.max(work, axis=0, keepdims=True)
        sel = jnp.min(jnp.where(work == m, eid, n_exp), axis=0, keepdims=True)
        hot = eid == sel
        vals.append(m)
        hots.append(hot)
        idx_ref[k:k + 1, :] = sel
        work = jnp.where(hot, -jnp.inf, work)
    ex = [jnp.exp(v - vals[0]) for v in vals]
    tot = ex[0] + ex[1] + ex[2] + ex[3]
    for k in range(TOP_K):
        gate_ref[k:k + 1, :] = ex[k] / tot

    chosen = (hots[0] | hots[1] | hots[2] | hots[3])
    a = jnp.where(chosen, 1.0, 0.0)
    earlier = (lax.broadcasted_iota(jnp.int32, (tm, tm), 0) < lax.broadcasted_iota(jnp.int32, (tm, tm), 1))
    prefix = jnp.dot(a.astype(BF16), jnp.where(earlier, 1.0, 0.0).astype(BF16),
                     preferred_element_type=F32) + carry[...]
    for k in range(TOP_K):
        rank_ref[k:k + 1, :] = jnp.sum(jnp.where(hots[k], prefix, 0.0), axis=0, keepdims=True).astype(jnp.int32)
    carry[...] = carry[...] + jnp.sum(a, axis=1, keepdims=True)
    cnt_ref[...] = carry[...].astype(jnp.int32)


def _two_group_specs(tm, d, npt):
    return [pl.BlockSpec((tm, d), lambda i: (jnp.minimum(i, npt - 1), 0)),
            pl.BlockSpec((tm, d), lambda i: (jnp.maximum(i - npt, 0), 0))]


def _router(x_p, x_s, g, w_router, b_router):
    d = x_p.shape[1]
    t = x_p.shape[0] + x_s.shape[0]
    n_exp = w_router.shape[1]
    tm = ROUTER_TM
    assert x_p.shape[0] % tm == 0 and x_s.shape[0] % tm == 0
    npt = x_p.shape[0] // tm
    row = pl.BlockSpec((TOP_K, tm), lambda i: (0, i))
    return pl.pallas_call(
        functools.partial(_router_body, n_exp=n_exp, n_prompt_tiles=npt),
        grid=(t // tm,),
        in_specs=[
            *_two_group_specs(tm, d, npt),
            _resident((1, d)),
            _resident((n_exp, d)),
            _resident((n_exp, 1)),
        ],
        out_specs=[pl.BlockSpec((tm, d), lambda i: (i, 0)), row, row, row,
                   pl.BlockSpec((n_exp, 1), lambda i: (0, 0))],
        out_shape=[
            jax.ShapeDtypeStruct((t, d), F32),
            jax.ShapeDtypeStruct((TOP_K, t), jnp.int32),
            jax.ShapeDtypeStruct((TOP_K, t), F32),
            jax.ShapeDtypeStruct((TOP_K, t), jnp.int32),
            jax.ShapeDtypeStruct((n_exp, 1), jnp.int32),
        ],
        scratch_shapes=[pltpu.VMEM((n_exp, 1), F32)],
        compiler_params=_params(("arbitrary",)),
        name="moe_router",
    )(x_p, x_s, g.reshape(1, d), w_router.astype(F32).T, b_router.astype(F32).reshape(n_exp, 1))


def _dispatch_body(pos_ref, fill_ref, x_ref, xs_ref, zeros, sem, fill_sem, *, n_exp):
    tm, d = x_ref.shape
    fill_rows = zeros.shape[0]

    def fill_copy(e):
        start = pl.multiple_of(fill_ref[e], SUBLANE)
        return pltpu.make_async_copy(zeros, xs_ref.at[pl.ds(start, fill_rows), :], fill_sem)

    @pl.when(pl.program_id(0) == 0)
    def _():
        zeros[...] = jnp.zeros_like(zeros)
        for e in range(n_exp):
            fill_copy(e).start()
        for e in range(n_exp):
            fill_copy(e).wait()

    def issue(t, _):
        for k in range(TOP_K):
            pltpu.make_async_copy(x_ref.at[pl.ds(t, 1), :], xs_ref.at[pl.ds(pos_ref[k, t], 1), :], sem).start(
                priority=k % 2)
        return 0

    lax.fori_loop(0, tm, issue, 0, unroll=8)
    for k in range(TOP_K):
        pltpu.make_async_copy(x_ref, xs_ref.at[pl.ds(0, tm), :], sem).wait()


def _dispatch(xn, pos, fill_start, n_rows):
    t, d = xn.shape
    tm = DISPATCH_TM
    n_exp = fill_start.shape[0]
    fill_rows = MOE_SB + SUBLANE
    return pl.pallas_call(
        functools.partial(_dispatch_body, n_exp=n_exp),
        grid=(t // tm,),
        in_specs=[
            pl.BlockSpec((TOP_K, tm), lambda i: (0, i), memory_space=pltpu.SMEM),
            pl.BlockSpec(memory_space=pltpu.SMEM),
            pl.BlockSpec((tm, d), lambda i: (i, 0)),
        ],
        out_specs=pl.BlockSpec(memory_space=pl.ANY),
        out_shape=jax.ShapeDtypeStruct((n_rows + fill_rows, d), xn.dtype),
        scratch_shapes=[pltpu.VMEM((fill_rows, d), xn.dtype), pltpu.SemaphoreType.DMA, pltpu.SemaphoreType.DMA],
        compiler_params=_params(("arbitrary",)),
        name="moe_dispatch",
    )(pos, fill_start, xn)


def _experts_body(te_ref, row0_ref, nv_ref, xs_ref, wg_ref, wl_ref, bg_ref, bl_ref, wd_ref, bd_ref,
                  ys_ref, acc, xb, stage, wgb, wlb, wdb, in_sem, out_sem, *, n_j, balanced_sub):
    del te_ref
    i, j = pl.program_id(0), pl.program_id(1)
    n_valid = nv_ref[i]
    row0 = row0_ref[i]
    d = acc.shape[1]
    n_sub = (n_valid + MOE_SB - 1) // MOE_SB
    n_big = n_valid // MOE_BIG
    small0 = n_big * MOE_BIG
    n_small = n_sub - n_big * (MOE_BIG // MOE_SB)

    def in_copy(s, slot):
        src = pl.ds(pl.multiple_of(row0 + s * MOE_SB, MOE_SB), MOE_SB)
        return pltpu.make_async_copy(xs_ref.at[src, :], stage.at[slot], in_sem.at[slot])

    def out_copy(first, size):
        src = pl.ds(pl.multiple_of(first, MOE_SB), size)
        dst = pl.ds(pl.multiple_of(row0 + first, MOE_SB), size)
        return pltpu.make_async_copy(acc.at[src, :], ys_ref.at[dst, :], out_sem)

    def drain(blocks):
        def one(s, _):
            out_copy(0, MOE_SB).wait()
            return 0

        lax.fori_loop(0, blocks, one, 0)

    n_tiles = pl.num_programs(0)
    has_next = (i + 1 < n_tiles) & (nv_ref[jnp.minimum(i + 1, n_tiles - 1)] > 0)
    prev_sub = (nv_ref[jnp.maximum(i - 1, 0)] + MOE_SB - 1) // MOE_SB

    def compute(first, size):
        rows = pl.ds(pl.multiple_of(first, MOE_SB), size)
        xs = xb[rows, :]
        hg = jnp.dot(xs, wgb[...], preferred_element_type=F32) + bg_ref[...]
        hl = jnp.dot(xs, wlb[...], preferred_element_type=F32) + bl_ref[...]
        glu = jnp.minimum(hg, SWIGLU_LIMIT)
        lin = jnp.clip(hl, -SWIGLU_LIMIT, SWIGLU_LIMIT)
        act = (glu * jax.nn.sigmoid(SWIGLU_ALPHA * glu) * (lin + 1.0)).astype(BF16)
        for c in range(0, d, MOE_DOWN_COLS):
            cols = slice(c, c + MOE_DOWN_COLS)
            acc[rows, cols] += jnp.dot(act, wdb[:, cols], preferred_element_type=F32)

        @pl.when(j == n_j - 1)
        def _():
            out_copy(first, size).start()

    @pl.when(n_valid > 0)
    def _():
        @pl.when(j == 0)
        def _():
            in_copy(0, 0).start()

            @pl.when(i > 0)
            def _():
                drain(prev_sub)

            def load(s, _):
                slot = s % 2
                rows = pl.ds(pl.multiple_of(s * MOE_SB, MOE_SB), MOE_SB)

                @pl.when(s + 1 < n_sub)
                def _():
                    in_copy(s + 1, 1 - slot).start()

                in_copy(s, slot).wait()
                xb[rows, :] = stage[slot].astype(BF16)
                acc[rows, :] = jnp.broadcast_to(bd_ref[...], (MOE_SB, d))
                return 0

            lax.fori_loop(0, n_sub, load, 0)

        wgb[...] = wg_ref[...].astype(BF16)
        wlb[...] = wl_ref[...].astype(BF16)
        wdb[...] = wd_ref[...].astype(BF16)

        def big(s, _):
            compute(s * MOE_BIG, MOE_BIG)
            return 0

        def small(s, _):
            compute(small0 + s * MOE_SB, MOE_SB)
            return 0

        balanced = n_sub == balanced_sub

        @pl.when(balanced)
        def _():
            compute(0, balanced_sub * MOE_SB)

        @pl.when(jnp.logical_not(balanced))
        def _():
            lax.fori_loop(0, n_big, big, 0)
            lax.fori_loop(0, n_small, small, 0)

        @pl.when((j == n_j - 1) & jnp.logical_not(has_next))
        def _():
            drain(n_sub)


def _experts(xs, tile_expert, tile_row0, tile_valid, n_rows, n_assign, w_up, b_up, w_down, b_down):
    d = xs.shape[1]
    n_tiles = tile_expert.shape[0]
    depth, n_exp, _, ff2 = w_up.shape
    assert depth == 1
    balanced_sub = min(-(-n_assign // (n_exp * MOE_SB)), MOE_CAP // MOE_SB)
    d_ff = ff2 // 2
    n_j = d_ff // MOE_TF
    b_up = b_up.reshape(depth, n_exp, 1, ff2)
    b_down = b_down.reshape(depth, n_exp, 1, d)

    def jsel(i, j, nv):
        return jnp.where(nv[i] > 0, j, n_j - 1)

    grid_spec = pltpu.PrefetchScalarGridSpec(
        num_scalar_prefetch=3,
        grid=(n_tiles, n_j),
        in_specs=[
            pl.BlockSpec(memory_space=pl.ANY),
            pl.BlockSpec((None, None, d, MOE_TF), lambda i, j, te, r0, nv: (0, te[i], 0, jsel(i, j, nv))),
            pl.BlockSpec((None, None, d, MOE_TF), lambda i, j, te, r0, nv: (0, te[i], 0, n_j + jsel(i, j, nv))),
            pl.BlockSpec((None, None, 1, MOE_TF), lambda i, j, te, r0, nv: (0, te[i], 0, jsel(i, j, nv))),
            pl.BlockSpec((None, None, 1, MOE_TF), lambda i, j, te, r0, nv: (0, te[i], 0, n_j + jsel(i, j, nv))),
            pl.BlockSpec((None, None, MOE_TF, d), lambda i, j, te, r0, nv: (0, te[i], jsel(i, j, nv), 0)),
            pl.BlockSpec((None, None, 1, d), lambda i, j, te, r0, nv: (0, te[i], 0, 0)),
        ],
        out_specs=pl.BlockSpec(memory_space=pl.ANY),
        scratch_shapes=[
            pltpu.VMEM((MOE_CAP, d), F32),
            pltpu.VMEM((MOE_CAP, d), BF16),
            pltpu.VMEM((2, MOE_SB, d), F32),
            pltpu.VMEM((d, MOE_TF), BF16),
            pltpu.VMEM((d, MOE_TF), BF16),
            pltpu.VMEM((MOE_TF, d), BF16),
            pltpu.SemaphoreType.DMA((2,)),
            pltpu.SemaphoreType.DMA,
        ],
    )
    return pl.pallas_call(
        functools.partial(_experts_body, n_j=n_j, balanced_sub=balanced_sub),
        grid_spec=grid_spec,
        out_shape=jax.ShapeDtypeStruct((n_rows, d), F32),
        compiler_params=_params(("arbitrary", "arbitrary")),
        name="moe_experts",
    )(tile_expert, tile_row0, tile_valid, xs, w_up, w_up, b_up, b_up, w_down, b_down)


def _combine_body(pos_ref, nxt_ref, xp_ref, xs_ref, gate_ref, g_ref, ys_ref, op_ref, os_ref, buf, sem,
                  *, n_prompt_tiles):
    i = pl.program_id(0)
    n = pl.num_programs(0)
    tm = xp_ref.shape[0]

    def gather(p_ref, slot):
        def issue(t, _):
            for k in range(TOP_K):
                pltpu.make_async_copy(ys_ref.at[pl.ds(p_ref[k, t], 1), :], buf.at[slot, k, pl.ds(t, 1), :],
                                      sem.at[slot]).start(priority=k % 2)
            return 0

        lax.fori_loop(0, tm, issue, 0, unroll=8)

    slot = i % 2

    @pl.when(i == 0)
    def _():
        gather(pos_ref, 0)

    @pl.when(i + 1 < n)
    def _():
        gather(nxt_ref, 1 - slot)

    for k in range(TOP_K):
        pltpu.make_async_copy(ys_ref.at[pl.ds(0, tm), :], buf.at[slot, k], sem.at[slot]).wait()

    out = jnp.where(i < n_prompt_tiles, xp_ref[...], xs_ref[...])
    for k in range(TOP_K):
        out = out + gate_ref[:, k:k + 1] * buf[slot, k]
    y = _rms(out, g_ref[...])

    @pl.when(i < n_prompt_tiles)
    def _():
        op_ref[...] = y

    @pl.when(i >= n_prompt_tiles)
    def _():
        os_ref[...] = y


def _combine(x_p, x_s, ys, pos, gate_t, g_final):
    n_prompt, d = x_p.shape
    t = n_prompt + x_s.shape[0]
    tm = TOK_TM
    assert n_prompt % tm == 0 and t % tm == 0
    npt = n_prompt // tm
    n = t // tm
    return pl.pallas_call(
        functools.partial(_combine_body, n_prompt_tiles=npt),
        grid=(n,),
        in_specs=[
            pl.BlockSpec((TOP_K, tm), lambda i: (0, i), memory_space=pltpu.SMEM),
            pl.BlockSpec((TOP_K, tm), lambda i: (0, jnp.minimum(i + 1, n - 1)), memory_space=pltpu.SMEM),
            *_two_group_specs(tm, d, npt),
            pl.BlockSpec((tm, TOP_K), lambda i: (i, 0)),
            _resident((1, d)),
            pl.BlockSpec(memory_space=pl.ANY),
        ],
        out_specs=_two_group_specs(tm, d, npt),
        out_shape=[
            jax.ShapeDtypeStruct((n_prompt, d), F32),
            jax.ShapeDtypeStruct((t - n_prompt, d), F32),
        ],
        scratch_shapes=[pltpu.VMEM((2, TOP_K, tm, d), F32), pltpu.SemaphoreType.DMA((2,))],
        compiler_params=_params(("arbitrary",)),
        name="moe_combine",
    )(pos, pos, x_p, x_s, gate_t, g_final.reshape(1, d), ys)


def _moe_layout(idx, rank, counts, n_tiles):
    n_exp = counts.shape[0]
    region = ((counts + MOE_SB - 1) // MOE_SB) * MOE_SB
    row_start = jnp.cumsum(region) - region
    tiles_per_exp = (counts + MOE_CAP - 1) // MOE_CAP
    tile_end = jnp.cumsum(tiles_per_exp)
    tile_start = tile_end - tiles_per_exp
    n_used = tile_end[-1]
    tiles = jnp.arange(n_tiles, dtype=jnp.int32)
    te = jnp.minimum(jnp.sum(tiles[:, None] >= tile_end[None, :], axis=1), n_exp - 1).astype(jnp.int32)
    te = jnp.where(tiles < n_used, te, te[jnp.maximum(n_used - 1, 0)])
    first = (tiles - tile_start[te]) * MOE_CAP
    valid = jnp.where(tiles < n_used, jnp.clip(counts[te] - first, 0, MOE_CAP), 0)
    row0 = jnp.where(tiles < n_used, row_start[te] + first, 0)
    hot = idx[..., None] == jnp.arange(n_exp, dtype=jnp.int32)
    pos = rank + jnp.sum(jnp.where(hot, row_start, 0), axis=-1)
    fill_start = row_start + (counts // SUBLANE) * SUBLANE
    i32 = lambda a: a.astype(jnp.int32)
    return i32(pos), te, i32(row0), i32(valid), i32(fill_start)


def kernel(x_prompt, x_sample, state_ssm_re, state_ssm_im, cache_swa_k, cache_swa_v, cache_mem_k, cache_mem_v, mem_prompt, w_in, ssm_lam_re, ssm_lam_im, ssm_log_dt, ssm_b_re, ssm_b_im, ssm_c_re, ssm_c_im, ssm_d, w_glu, swa_sinks, rel_bias, w_o_attn, w_out, w_xq, w_mk, w_mv, w_xo, w_router, b_router, w_up, b_up, w_down, b_down, norm_mix, norm_x, norm_mem, norm_moe, norm_final):
    depth = w_in.shape[0]
    assert depth == 1, "single-layer step"
    bp, lp, d = x_prompt.shape
    bs, ls, _ = x_sample.shape
    n_p, n_s = bp * lp, bs * ls
    t = n_p + n_s
    _, _, window, kvh, hd = cache_swa_k.shape
    kv_width = kvh * hd
    n_heads = swa_sinks.shape[1]
    q_width = n_heads * hd
    ssm_width = ssm_d.shape[1]
    _, _, mem_len, x_heads, x_hd = cache_mem_k.shape
    xw = x_heads * x_hd
    ssm_groups, ssm_state = ssm_lam_re.shape[1], ssm_lam_re.shape[2]
    q_col = ssm_width
    k_col = q_col + q_width
    v_col = k_col + kv_width
    gs_col = v_col + kv_width
    ga_col = gs_col + d

    l = 0
    w_in_b = w_in[l].astype(BF16)
    w_glu_b, w_o_b, w_out_b = w_glu[l].astype(BF16), w_o_attn[l].astype(BF16), w_out[l].astype(BF16)
    w_xq_b, w_xo_b = w_xq[l].astype(BF16), w_xo[l].astype(BF16)
    w_mkv_b = jnp.concatenate([w_mk[l], w_mv[l]], axis=1).astype(BF16)
    a_re, a_im, bcat, ccat = _s5_discretize(ssm_lam_re[l], ssm_lam_im[l], ssm_log_dt[l], ssm_b_re[l],
                                            ssm_b_im[l], ssm_c_re[l], ssm_c_im[l])
    bucket = _t5_bucket_table(CHUNK, window + CHUNK, window)
    bias = jnp.transpose(rel_bias.astype(F32)[bucket], (2, 0, 1))

    def mixers(x, n_seq, seq_len, s0_re, s0_im, k_cache, v_cache):
        xf = x.reshape(n_seq * seq_len, d)
        h = _norm_matmul(xf, norm_mix[l], w_in_b, tm=1024, tn=1664)
        y_ssm, s_re, s_im = _s5(h[:, :ssm_width], s0_re, s0_im, a_re, a_im, bcat, ccat, ssm_d[l],
                                n_seq=n_seq, seq_len=seq_len)
        attn = _swa(h, k_cache, v_cache, swa_sinks[l], bias, n_seq=n_seq, seq_len=seq_len, q_col=q_col,
                    k_col=k_col, v_col=v_col, q_width=q_width, kv_width=kv_width, window=window)
        x1 = _merge(xf, y_ssm, attn, h, w_glu_b, w_o_b, w_out_b, gs_col=gs_col, ga_col=ga_col, tm=256)
        h3 = h.reshape(n_seq, seq_len, -1)
        k_new = h3[:, :, k_col:k_col + kv_width]
        v_new = h3[:, :, v_col:v_col + kv_width]
        if k_cache is not None:
            k_new = jnp.concatenate([k_cache.reshape(n_seq, window, kv_width), k_new], axis=1)
            v_new = jnp.concatenate([v_cache.reshape(n_seq, window, kv_width), v_new], axis=1)
        k_new = k_new[:, -window:].reshape(1, n_seq, window, kvh, hd)
        v_new = v_new[:, -window:].reshape(1, n_seq, window, kvh, hd)
        state_shape = (1, n_seq, ssm_groups, ssm_state)
        return x1, s_re.reshape(state_shape), s_im.reshape(state_shape), k_new, v_new

    zero_state = jnp.zeros((bp, ssm_groups * ssm_state), F32)
    x1_p, sre_p, sim_p, k_p, v_p = mixers(x_prompt, bp, lp, zero_state, zero_state, None, None)
    x1_s, sre_s, sim_s, k_s, v_s = mixers(x_sample, bs, ls, state_ssm_re[l], state_ssm_im[l],
                                          cache_swa_k[l], cache_swa_v[l])

    mkv = _norm_matmul(mem_prompt.reshape(bp * mem_len, d), norm_mem[l], w_mkv_b, tm=1024, tn=512)
    mkv = mkv.reshape(bp, mem_len, 2 * xw)
    mk_p, mv_p = mkv[:, :, :xw], mkv[:, :, xw:]

    x2_p = _xattn(x1_p, norm_x[l], mk_p, mv_p, w_xq_b, w_xo_b, n_seq=bp, seq_len=lp, n_heads=x_heads)
    x2_s = _xattn(x1_s, norm_x[l], cache_mem_k[l].reshape(bs, mem_len, xw), cache_mem_v[l].reshape(bs, mem_len, xw),
                  w_xq_b, w_xo_b, n_seq=bs, seq_len=ls, n_heads=x_heads)

    n_exp = w_router.shape[2]
    xn, idx, gate, rank, counts = _router(x2_p, x2_s, norm_moe[l], w_router[l], b_router[l])
    n_tiles = (t * TOP_K) // MOE_CAP + n_exp
    n_rows = t * TOP_K + n_exp * MOE_SB
    pos, tile_expert, tile_row0, tile_valid, fill_start = _moe_layout(idx, rank, counts[:, 0], n_tiles)
    xs = _dispatch(xn, pos, fill_start, n_rows)
    ys = _experts(xs, tile_expert, tile_row0, tile_valid, n_rows, t * TOP_K, w_up, b_up, w_down, b_down)
    y_p, y_s = _combine(x2_p, x2_s, ys, pos, gate.T, norm_final)

    return (y_p.reshape(bp, lp, d), y_s.reshape(bs, ls, d), sre_p, sim_p, k_p, v_p,
            mk_p.reshape(1, bp, mem_len, x_heads, x_hd), mv_p.reshape(1, bp, mem_len, x_heads, x_hd),
            sre_s, sim_s, k_s, v_s)
```
